```python
import jax, jax.numpy as jnp
from jax import lax
import numpy as np

D_MODEL = 2048
BATCH = 1
SEQ = 16384
DEPTH = 1

HEAD_DIM = 64
N_Q_HEADS = 16
N_KV_HEADS = 4
Q_PER_KV = N_Q_HEADS // N_KV_HEADS
ATTN_WIDTH = N_Q_HEADS * HEAD_DIM
KV_WIDTH = N_KV_HEADS * HEAD_DIM
WINDOW = 128
BLOCK = 128
ROPE_THETA = 500000.0
ROT_DIM = HEAD_DIM // 4

CONV_WIDTH = D_MODEL // 2
CONV_K = 3

N_MEM = 256
X_HEADS = 4
X_HEAD_DIM = 128
X_WIDTH = X_HEADS * X_HEAD_DIM

D_FF = -(-8 * D_MODEL // (3 * 256)) * 256

IN_SIZES = (ATTN_WIDTH, KV_WIDTH, KV_WIDTH, CONV_WIDTH, CONV_WIDTH, CONV_WIDTH, D_MODEL, D_MODEL)
IN_WIDTH = sum(IN_SIZES)

EPS = 1e-6

kernel_name = "hybrid_swa_sink_shortconv_gated_merge"


def rms_norm(x, g):
    xf = x.astype(jnp.float32)
    y = xf * lax.rsqrt(jnp.mean(xf * xf, axis=-1, keepdims=True) + EPS)
    return (y * g.astype(jnp.float32)).astype(x.dtype)


def partial_rope(t, pos):
    half = ROT_DIM // 2
    inv_freq = ROPE_THETA ** (-jnp.arange(0, ROT_DIM, 2, dtype=jnp.float32) / ROT_DIM)
    ang = pos.astype(jnp.float32)[:, None] * inv_freq[None, :]
    cos = jnp.cos(ang)[None, :, None, :]
    sin = jnp.sin(ang)[None, :, None, :]
    tr = t[..., :ROT_DIM].astype(jnp.float32)
    t1, t2 = tr[..., :half], tr[..., half:]
    rot = jnp.concatenate([t1 * cos - t2 * sin, t2 * cos + t1 * sin], axis=-1).astype(t.dtype)
    return jnp.concatenate([rot, t[..., ROT_DIM:]], axis=-1)


def sliding_window_sink_attention(q, k, v, sinks):
    B, S = q.shape[0], q.shape[1]
    nb = S // BLOCK
    qb = q.reshape(B, nb, BLOCK, N_KV_HEADS, Q_PER_KV, HEAD_DIM)

    def with_prev(t):
        tb = t.reshape(B, nb, BLOCK, N_KV_HEADS, HEAD_DIM)
        prev = jnp.pad(tb[:, :-1], ((0, 0), (1, 0), (0, 0), (0, 0), (0, 0)))
        return jnp.concatenate([prev, tb], axis=2)

    kb, vb = with_prev(k), with_prev(v)
    scale = HEAD_DIM ** -0.5
    s = jnp.einsum('bnqkgd,bnckd->bkgnqc', qb, kb).astype(jnp.float32) * scale
    qi = jnp.arange(BLOCK)[:, None]
    c = jnp.arange(2 * BLOCK)[None, :]
    rel = qi + BLOCK - c
    band = (rel >= 0) & (rel < WINDOW)
    not_first = (jnp.arange(nb) > 0)[:, None, None]
    valid = band[None] & (not_first | (c >= BLOCK)[None])
    s = jnp.where(valid, s, -jnp.inf)
    sink = sinks.astype(jnp.float32).reshape(N_KV_HEADS, Q_PER_KV)[None, :, :, None, None, None]
    m = jnp.maximum(jnp.max(s, axis=-1, keepdims=True), sink)
    p = jnp.exp(s - m)
    denom = jnp.sum(p, axis=-1, keepdims=True) + jnp.exp(sink - m)
    w = (p / denom).astype(v.dtype)
    o = jnp.einsum('bkgnqc,bnckd->bnqkgd', w, vb)
    return o.reshape(B, S, ATTN_WIDTH)


def gated_short_conv(u, b_gate, c_gate, conv_w):
    S = u.shape[1]
    v = c_gate * u
    vp = jnp.pad(v, ((0, 0), (CONV_K - 1, 0), (0, 0)))
    y = conv_w[0] * vp[:, 0:S]
    for tap in range(1, CONV_K):
        y = y + conv_w[tap] * vp[:, tap:tap + S]
    return b_gate * y


def memory_cross_attention(h, mem_n, w_xq, w_xkv, w_xo):
    B, S = h.shape[0], h.shape[1]
    q = (h @ w_xq).reshape(B, S, X_HEADS, X_HEAD_DIM)
    kv = mem_n @ w_xkv
    k = kv[..., :X_WIDTH].reshape(B, -1, X_HEADS, X_HEAD_DIM)
    v = kv[..., X_WIDTH:].reshape(B, -1, X_HEADS, X_HEAD_DIM)
    s = jnp.einsum('bshd,bmhd->bhsm', q, k).astype(jnp.float32) * (X_HEAD_DIM ** -0.5)
    p = jax.nn.softmax(s, axis=-1).astype(v.dtype)
    o = jnp.einsum('bhsm,bmhd->bshd', p, v).reshape(B, S, X_WIDTH)
    return o @ w_xo


def swiglu(h, w_gate_up, w_down):
    gu = h @ w_gate_up
    g, u = gu[..., :D_FF], gu[..., D_FF:]
    return (jax.nn.silu(g) * u) @ w_down


def setup_inputs(seed: int = 0) -> dict:
    key = jax.random.key(seed)
    ks = jax.random.split(key, 20)
    f32 = jnp.float32

    def nrm(k, shape, scale):
        return jax.random.normal(k, shape, f32) * scale

    L, D = DEPTH, D_MODEL
    return {
        "x": nrm(ks[0], (BATCH, SEQ, D), 1.0),
        "mem": nrm(ks[1], (BATCH, N_MEM, D), 1.0),
        "g_mix": 1.0 + nrm(ks[2], (L, D), 0.02),
        "w_in": nrm(ks[3], (L, D, IN_WIDTH), D ** -0.5),
        "conv_w": nrm(ks[4], (L, CONV_K, CONV_WIDTH), CONV_K ** -0.5),
        "sinks": nrm(ks[5], (L, N_Q_HEADS), 0.5),
        "w_attn_o": nrm(ks[6], (L, ATTN_WIDTH, D), ATTN_WIDTH ** -0.5),
        "w_conv_o": nrm(ks[7], (L, CONV_WIDTH, D), CONV_WIDTH ** -0.5),
        "w_out": nrm(ks[8], (L, D, D), D ** -0.5),
        "g_xattn": 1.0 + nrm(ks[9], (L, D), 0.02),
        "g_mem": 1.0 + nrm(ks[10], (L, D), 0.02),
        "w_xq": nrm(ks[11], (L, D, X_WIDTH), D ** -0.5),
        "w_xkv": nrm(ks[12], (L, D, 2 * X_WIDTH), D ** -0.5),
        "w_xo": nrm(ks[13], (L, X_WIDTH, D), X_WIDTH ** -0.5),
        "g_ffn": 1.0 + nrm(ks[14], (L, D), 0.02),
        "w_gate_up": nrm(ks[15], (L, D, 2 * D_FF), D ** -0.5),
        "w_down": nrm(ks[16], (L, D_FF, D), D_FF ** -0.5),
        "g_final": 1.0 + nrm(ks[17], (D,), 0.02),
    }


def reference(x, mem, g_mix, w_in, conv_w, sinks, w_attn_o, w_conv_o, w_out,
              g_xattn, g_mem, w_xq, w_xkv, w_xo, g_ffn, w_gate_up, w_down, g_final):
    B, S = x.shape[0], x.shape[1]
    pos = jnp.arange(S, dtype=jnp.int32)
    h = x
    for l in range(DEPTH):
        n = rms_norm(h, g_mix[l])
        z = n @ w_in[l]
        parts = []
        off = 0
        for size in IN_SIZES:
            parts.append(z[..., off:off + size])
            off += size
        q, k, v, cb, cc, ch, ga, gc = parts
        q = partial_rope(q.reshape(B, S, N_Q_HEADS, HEAD_DIM), pos)
        k = partial_rope(k.reshape(B, S, N_KV_HEADS, HEAD_DIM), pos)
        v = v.reshape(B, S, N_KV_HEADS, HEAD_DIM)
        y_attn = sliding_window_sink_attention(q, k, v, sinks[l]) @ w_attn_o[l]
        y_conv = gated_short_conv(ch, cb, cc, conv_w[l]) @ w_conv_o[l]
        merged = jax.nn.sigmoid(ga) * y_attn + jax.nn.sigmoid(gc) * y_conv
        h = h + merged @ w_out[l]
        mem_n = rms_norm(mem, g_mem[l])
        h = h + memory_cross_attention(rms_norm(h, g_xattn[l]), mem_n, w_xq[l], w_xkv[l], w_xo[l])
        h = h + swiglu(rms_norm(h, g_ffn[l]), w_gate_up[l], w_down[l])
    return rms_norm(h, g_final)
```

```python
import functools

import jax
import jax.numpy as jnp
from jax import lax
from jax.experimental import pallas as pl
from jax.experimental.pallas import tpu as pltpu

F32 = jnp.float32
BF16 = jnp.bfloat16

EPS = 1e-6
HEAD_DIM = 64
N_Q_HEADS = 16
N_KV_HEADS = 4
Q_PER_KV = N_Q_HEADS // N_KV_HEADS
BLOCK = 128
ROT_DIM = HEAD_DIM // 4
ROPE_THETA = 500000.0
CONV_K = 3
X_HEADS = 4
X_HEAD_DIM = 128
LANES = 128
SUBLANES = 8

VMEM_LIMIT = 56 * 1024 * 1024


def _rms(x, g):
    ms = jnp.mean(x * x, axis=-1, keepdims=True)
    return x * lax.rsqrt(ms + EPS) * g


def _sigmoid(x):
    return 0.5 * jnp.tanh(0.5 * x) + 0.5


def _mem_kv_kernel(mem_ref, g_ref, w_ref, o_ref):
    n = _rms(mem_ref[...], g_ref[...]).astype(BF16)
    o_ref[...] = jnp.dot(n, w_ref[...], preferred_element_type=F32).astype(o_ref.dtype)


def _mem_kv(mem, g_mem, w_xkv):
    n_mem, d = mem.shape
    return pl.pallas_call(
        _mem_kv_kernel,
        out_shape=jax.ShapeDtypeStruct((n_mem, w_xkv.shape[1]), BF16),
        compiler_params=pltpu.CompilerParams(vmem_limit_bytes=VMEM_LIMIT),
        name="mem_kv",
    )(mem, g_mem.reshape(1, d), w_xkv)


def _in_proj_kernel(x_ref, g_ref, w_ref, o_ref, n_ref):
    @pl.when(pl.program_id(1) == 0)
    def _():
        n_ref[...] = _rms(x_ref[...], g_ref[...]).astype(BF16)

    o_ref[...] = jnp.dot(n_ref[...], w_ref[...], preferred_element_type=F32).astype(o_ref.dtype)


def _in_proj(x, g, w, tm, tn):
    s, d = x.shape
    n = w.shape[1]
    return pl.pallas_call(
        _in_proj_kernel,
        grid=(s // tm, n // tn),
        in_specs=[
            pl.BlockSpec((tm, d), lambda i, j: (i, 0)),
            pl.BlockSpec((1, d), lambda i, j: (0, 0)),
            pl.BlockSpec((d, tn), lambda i, j: (0, j)),
        ],
        out_specs=pl.BlockSpec((tm, tn), lambda i, j: (i, j)),
        out_shape=jax.ShapeDtypeStruct((s, n), BF16),
        scratch_shapes=[pltpu.VMEM((tm, d), BF16)],
        compiler_params=pltpu.CompilerParams(
            dimension_semantics=("arbitrary", "arbitrary"), vmem_limit_bytes=VMEM_LIMIT),
        name="in_proj",
    )(x, g.reshape(1, d), w)


def _mixer_kernel(q_ref, cb_ref, cc_ref, ch_ref, ga_ref, gc_ref, kv_ref, x_ref,
                  cos_ref, sa_ref, sb_ref, sink_ref, convw_ref, wao_ref, wco_ref, wout_ref,
                  o_ref, qbuf, kbuf, vbuf, cbuf, abuf):
    i = pl.program_id(0)
    tm = q_ref.shape[0]
    nblk = tm // BLOCK
    kvw = N_KV_HEADS * HEAD_DIM

    @pl.when(i == 0)
    def _():
        kbuf[0:BLOCK, :] = jnp.zeros((BLOCK, kvw), BF16)
        vbuf[0:BLOCK, :] = jnp.zeros((BLOCK, kvw), BF16)
        cbuf[0:SUBLANES, :] = jnp.zeros((SUBLANES, cbuf.shape[1]), F32)

    @pl.when(i > 0)
    def _():
        kbuf[0:BLOCK, :] = kbuf[tm:tm + BLOCK, :]
        vbuf[0:BLOCK, :] = vbuf[tm:tm + BLOCK, :]
        cbuf[0:SUBLANES, :] = cbuf[tm:tm + SUBLANES, :]

    cos = cos_ref[...]
    sa = sa_ref[...]
    sb = sb_ref[...]
    half = ROT_DIM // 2

    def rope(t):
        return t * cos + pltpu.roll(t, LANES - half, 1) * sa + pltpu.roll(t, half, 1) * sb

    scale = HEAD_DIM ** -0.5
    for c in range(q_ref.shape[1] // LANES):
        sl = slice(c * LANES, (c + 1) * LANES)
        qbuf[:, sl] = (rope(q_ref[:, sl].astype(F32)) * scale).astype(BF16)
    for c in range(kvw // LANES):
        sl = slice(c * LANES, (c + 1) * LANES)
        kbuf[BLOCK:, sl] = rope(kv_ref[:, sl].astype(F32)).astype(BF16)
    vbuf[BLOCK:, :] = kv_ref[:, kvw:2 * kvw]

    qi = lax.broadcasted_iota(jnp.int32, (BLOCK, 2 * BLOCK), 0)
    ci = lax.broadcasted_iota(jnp.int32, (BLOCK, 2 * BLOCK), 1)
    rel = qi + BLOCK - ci
    band = (rel >= 0) & (rel < BLOCK)

    def attn_block(b, carry):
        r0 = pl.multiple_of(b * BLOCK, BLOCK)
        first = jnp.logical_and(i == 0, b == 0)
        lim = jnp.where(first, BLOCK, 0)
        mask = band & (ci >= lim)
        for g in range(N_KV_HEADS):
            kc = kbuf[pl.ds(r0, 2 * BLOCK), g * HEAD_DIM:(g + 1) * HEAD_DIM]
            vc = vbuf[pl.ds(r0, 2 * BLOCK), g * HEAD_DIM:(g + 1) * HEAD_DIM]
            for r in range(Q_PER_KV):
                h = g * Q_PER_KV + r
                qh = qbuf[pl.ds(r0, BLOCK), h * HEAD_DIM:(h + 1) * HEAD_DIM]
                s = lax.dot_general(qh, kc, (((1,), (1,)), ((), ())), preferred_element_type=F32)
                s = jnp.where(mask, s, -jnp.inf)
                sink = sink_ref[h]
                m = jnp.maximum(jnp.max(s, axis=-1, keepdims=True), sink)
                p = jnp.exp(s - m)
                denom = jnp.sum(p, axis=-1, keepdims=True) + jnp.exp(sink - m)
                w = (p / denom).astype(BF16)
                o = jnp.dot(w, vc, preferred_element_type=F32)
                abuf[pl.ds(r0, BLOCK), h * HEAD_DIM:(h + 1) * HEAD_DIM] = o.astype(BF16)
        return carry

    lax.fori_loop(0, nblk, attn_block, 0)

    cbuf[SUBLANES:, :] = cc_ref[...].astype(F32) * ch_ref[...].astype(F32)
    cw = convw_ref[...]
    y = cw[2:3, :] * cbuf[SUBLANES:, :]
    y = y + cw[1:2, :] * cbuf[SUBLANES - 1:SUBLANES - 1 + tm, :]
    y = y + cw[0:1, :] * cbuf[SUBLANES - 2:SUBLANES - 2 + tm, :]
    conv = (cb_ref[...].astype(F32) * y).astype(BF16)

    ya = jnp.dot(abuf[...], wao_ref[...], preferred_element_type=F32)
    yc = jnp.dot(conv, wco_ref[...], preferred_element_type=F32)
    merged = _sigmoid(ga_ref[...].astype(F32)) * ya + _sigmoid(gc_ref[...].astype(F32)) * yc
    o_ref[...] = x_ref[...] + jnp.dot(merged.astype(BF16), wout_ref[...], preferred_element_type=F32)


def _const_spec(shape):
    return pl.BlockSpec(shape, lambda i: (0,) * len(shape), pipeline_mode=pl.Buffered(1))


def _mixer(z, x, cos_t, sa_t, sb_t, sinks, conv_w, w_attn_o, w_conv_o, w_out, tm):
    s, d = x.shape
    aw = N_Q_HEADS * HEAD_DIM
    kvw = N_KV_HEADS * HEAD_DIM
    cw = conv_w.shape[1]
    assert aw == cw and d == 2 * aw
    kv_blk = (aw + 3 * cw + 2 * d) // (2 * kvw)
    in_specs = [
        pl.BlockSpec((tm, aw), lambda i: (i, 0)),
        pl.BlockSpec((tm, cw), lambda i: (i, 1)),
        pl.BlockSpec((tm, cw), lambda i: (i, 2)),
        pl.BlockSpec((tm, cw), lambda i: (i, 3)),
        pl.BlockSpec((tm, d), lambda i: (i, 2)),
        pl.BlockSpec((tm, d), lambda i: (i, 3)),
        pl.BlockSpec((tm, 2 * kvw), lambda i: (i, kv_blk)),
        pl.BlockSpec((tm, d), lambda i: (i, 0)),
        pl.BlockSpec((tm, LANES), lambda i: (i, 0)),
        pl.BlockSpec((tm, LANES), lambda i: (i, 0)),
        pl.BlockSpec((tm, LANES), lambda i: (i, 0)),
        pl.BlockSpec(memory_space=pltpu.SMEM),
        _const_spec((CONV_K, cw)),
        _const_spec((aw, d)),
        _const_spec((cw, d)),
        _const_spec((d, d)),
    ]
    return pl.pallas_call(
        _mixer_kernel,
        grid=(s // tm,),
        in_specs=in_specs,
        out_specs=pl.BlockSpec((tm, d), lambda i: (i, 0)),
        out_shape=jax.ShapeDtypeStruct((s, d), F32),
        scratch_shapes=[
            pltpu.VMEM((tm, aw), BF16),
            pltpu.VMEM((tm + BLOCK, kvw), BF16),
            pltpu.VMEM((tm + BLOCK, kvw), BF16),
            pltpu.VMEM((tm + SUBLANES, cw), F32),
            pltpu.VMEM((tm, aw), BF16),
        ],
        compiler_params=pltpu.CompilerParams(
            dimension_semantics=("arbitrary",), vmem_limit_bytes=VMEM_LIMIT),
        name="mixer",
    )(z, z, z, z, z, z, z, x, cos_t, sa_t, sb_t, sinks, conv_w, w_attn_o, w_conv_o, w_out)


def _xattn_kernel(h_ref, g_ref, wq_ref, kv_ref, wo_ref, o_ref, obuf):
    h = h_ref[...]
    n = _rms(h, g_ref[...]).astype(BF16)
    xw = X_HEADS * X_HEAD_DIM
    q = jnp.dot(n, wq_ref[...], preferred_element_type=F32) * (X_HEAD_DIM ** -0.5)
    q = q.astype(BF16)
    for hd in range(X_HEADS):
        sl = slice(hd * X_HEAD_DIM, (hd + 1) * X_HEAD_DIM)
        k = kv_ref[:, sl]
        v = kv_ref[:, xw + hd * X_HEAD_DIM: xw + (hd + 1) * X_HEAD_DIM]
        s = lax.dot_general(q[:, sl], k, (((1,), (1,)), ((), ())), preferred_element_type=F32)
        m = jnp.max(s, axis=-1, keepdims=True)
        p = jnp.exp(s - m)
        p = (p / jnp.sum(p, axis=-1, keepdims=True)).astype(BF16)
        obuf[:, sl] = jnp.dot(p, v, preferred_element_type=F32).astype(BF16)
    o_ref[...] = h + jnp.dot(obuf[...], wo_ref[...], preferred_element_type=F32)


def _xattn(h, g, w_xq, kv_mem, w_xo, tm):
    s, d = h.shape
    xw = w_xq.shape[1]
    return pl.pallas_call(
        _xattn_kernel,
        grid=(s // tm,),
        in_specs=[
            pl.BlockSpec((tm, d), lambda i: (i, 0)),
            _const_spec((1, d)),
            _const_spec((d, xw)),
            _const_spec(kv_mem.shape),
            _const_spec((xw, d)),
        ],
        out_specs=pl.BlockSpec((tm, d), lambda i: (i, 0)),
        out_shape=jax.ShapeDtypeStruct((s, d), F32),
        scratch_shapes=[pltpu.VMEM((tm, xw), BF16)],
        compiler_params=pltpu.CompilerParams(
            dimension_semantics=("arbitrary",), vmem_limit_bytes=VMEM_LIMIT),
        name="xattn",
    )(h, g.reshape(1, d), w_xq, kv_mem, w_xo)


def _ffn_kernel(h_ref, g_ref, wg_ref, wu_ref, wd_ref, gf_ref, o_ref, n_ref, *, final_norm):
    j = pl.program_id(1)

    @pl.when(j == 0)
    def _():
        h = h_ref[...]
        n_ref[...] = _rms(h, g_ref[...]).astype(BF16)
        o_ref[...] = h

    n = n_ref[...]
    g = jnp.dot(n, wg_ref[...], preferred_element_type=F32)
    u = jnp.dot(n, wu_ref[...], preferred_element_type=F32)
    act = (g * _sigmoid(g) * u).astype(BF16)
    o_ref[...] += jnp.dot(act, wd_ref[...], preferred_element_type=F32)

    if final_norm:
        @pl.when(j == pl.num_programs(1) - 1)
        def _():
            o_ref[...] = _rms(o_ref[...], gf_ref[...])


def _ffn(h, g, w_gate_up, w_down, g_final, tm, tf, final_norm):
    s, d = h.shape
    dff = w_down.shape[0]
    nf = dff // tf
    return pl.pallas_call(
        functools.partial(_ffn_kernel, final_norm=final_norm),
        grid=(s // tm, nf),
        in_specs=[
            pl.BlockSpec((tm, d), lambda i, j: (i, 0)),
            pl.BlockSpec((1, d), lambda i, j: (0, 0)),
            pl.BlockSpec((d, tf), lambda i, j: (0, j)),
            pl.BlockSpec((d, tf), lambda i, j: (0, j + nf)),
            pl.BlockSpec((tf, d), lambda i, j: (j, 0)),
            pl.BlockSpec((1, d), lambda i, j: (0, 0)),
        ],
        out_specs=pl.BlockSpec((tm, d), lambda i, j: (i, 0)),
        out_shape=jax.ShapeDtypeStruct((s, d), F32),
        scratch_shapes=[pltpu.VMEM((tm, d), BF16)],
        compiler_params=pltpu.CompilerParams(
            dimension_semantics=("arbitrary", "arbitrary"), vmem_limit_bytes=VMEM_LIMIT),
        name="ffn",
    )(h, g.reshape(1, d), w_gate_up, w_gate_up, w_down, g_final.reshape(1, d))


def _rope_tables(s):
    half = ROT_DIM // 2
    inv_freq = ROPE_THETA ** (-jnp.arange(0, ROT_DIM, 2, dtype=F32) / ROT_DIM)
    ang = jnp.arange(s, dtype=jnp.int32).astype(F32)[:, None] * inv_freq[None, :]
    cos, sin = jnp.cos(ang), jnp.sin(ang)
    ones = jnp.ones((s, HEAD_DIM - ROT_DIM), F32)
    zeros = jnp.zeros((s, HEAD_DIM - ROT_DIM), F32)
    zh = jnp.zeros((s, half), F32)
    cos_h = jnp.concatenate([cos, cos, ones], axis=1)
    sa_h = jnp.concatenate([-sin, zh, zeros], axis=1)
    sb_h = jnp.concatenate([zh, sin, zeros], axis=1)
    rep = LANES // HEAD_DIM
    return tuple(jnp.tile(t, (1, rep)) for t in (cos_h, sa_h, sb_h))


def kernel(x, mem, g_mix, w_in, conv_w, sinks, w_attn_o, w_conv_o, w_out, g_xattn, g_mem,
           w_xq, w_xkv, w_xo, g_ffn, w_gate_up, w_down, g_final):
    batch, s, d = x.shape
    depth = w_in.shape[0]
    aw = N_Q_HEADS * HEAD_DIM
    kvw = N_KV_HEADS * HEAD_DIM
    cw = conv_w.shape[2]
    cos_t, sa_t, sb_t = _rope_tables(s)
    o_k = aw
    o_cb = aw + 2 * kvw

    outs = []
    for b in range(batch):
        h = x[b]
        for l in range(depth):
            wl = w_in[l]
            w_in_p = jnp.concatenate([wl[:, :o_k], wl[:, o_cb:], wl[:, o_k:o_cb]], axis=1).astype(BF16)
            z = _in_proj(h, g_mix[l], w_in_p, tm=1024, tn=512)
            h = _mixer(z, h, cos_t, sa_t, sb_t, sinks[l], conv_w[l], w_attn_o[l].astype(BF16),
                       w_conv_o[l].astype(BF16), w_out[l].astype(BF16), tm=256)
            kv_mem = _mem_kv(mem[b], g_mem[l], w_xkv[l].astype(BF16))
            h = _xattn(h, g_xattn[l], w_xq[l].astype(BF16), kv_mem, w_xo[l].astype(BF16), tm=512)
            h = _ffn(h, g_ffn[l], w_gate_up[l].astype(BF16), w_down[l].astype(BF16), g_final,
                     tm=512, tf=512, final_norm=(l == depth - 1))
        if depth == 0:
            raise ValueError("depth must be positive")
        outs.append(h)
    return jnp.stack(outs, axis=0)
```

```python
import functools

import jax
import jax.numpy as jnp
from jax import lax
from jax.experimental import pallas as pl
from jax.experimental.pallas import tpu as pltpu

F32 = jnp.float32
BF16 = jnp.bfloat16

EPS = 1e-6
HEAD_DIM = 64
N_Q_HEADS = 16
N_KV_HEADS = 4
Q_PER_KV = N_Q_HEADS // N_KV_HEADS
BLOCK = 128
ROT_DIM = HEAD_DIM // 4
ROPE_THETA = 500000.0
CONV_K = 3
X_HEADS = 4
X_HEAD_DIM = 128
LANES = 128
SUBLANES = 8

VMEM_LIMIT = 56 * 1024 * 1024


def _rms(x, g):
    ms = jnp.mean(x * x, axis=-1, keepdims=True)
    return x * lax.rsqrt(ms + EPS) * g


def _sigmoid(x):
    return 0.5 * jnp.tanh(0.5 * x) + 0.5


def _mem_kv_kernel(mem_ref, g_ref, w_ref, o_ref):
    n = _rms(mem_ref[...], g_ref[...]).astype(BF16)
    o_ref[...] = jnp.dot(n, w_ref[...], preferred_element_type=F32).astype(o_ref.dtype)


def _mem_kv(mem, g_mem, w_xkv):
    n_mem, d = mem.shape
    return pl.pallas_call(
        _mem_kv_kernel,
        out_shape=jax.ShapeDtypeStruct((n_mem, w_xkv.shape[1]), BF16),
        compiler_params=pltpu.CompilerParams(vmem_limit_bytes=VMEM_LIMIT),
        name="mem_kv",
    )(mem, g_mem.reshape(1, d), w_xkv)


def _in_proj_kernel(x_ref, g_ref, w_ref, o_ref, n_ref):
    @pl.when(pl.program_id(1) == 0)
    def _():
        n_ref[...] = _rms(x_ref[...], g_ref[...]).astype(BF16)

    o_ref[...] = jnp.dot(n_ref[...], w_ref[...], preferred_element_type=F32).astype(o_ref.dtype)


def _in_proj(x, g, w, tm, tn):
    s, d = x.shape
    n = w.shape[1]
    return pl.pallas_call(
        _in_proj_kernel,
        grid=(s // tm, n // tn),
        in_specs=[
            pl.BlockSpec((tm, d), lambda i, j: (i, 0)),
            pl.BlockSpec((1, d), lambda i, j: (0, 0)),
            pl.BlockSpec((d, tn), lambda i, j: (0, j)),
        ],
        out_specs=pl.BlockSpec((tm, tn), lambda i, j: (i, j)),
        out_shape=jax.ShapeDtypeStruct((s, n), BF16),
        scratch_shapes=[pltpu.VMEM((tm, d), BF16)],
        compiler_params=pltpu.CompilerParams(
            dimension_semantics=("arbitrary", "arbitrary"), vmem_limit_bytes=VMEM_LIMIT),
        name="in_proj",
    )(x, g.reshape(1, d), w)


Z_BLK = 512
N_GATE_BLKS = 4


def _mixer_kernel(q_ref, kv_ref, cb0_ref, cb1_ref, cc0_ref, cc1_ref, ch0_ref, ch1_ref,
                  ga0_ref, ga1_ref, ga2_ref, ga3_ref, gc0_ref, gc1_ref, gc2_ref, gc3_ref,
                  x_ref, tab_ref, sink_ref, convw_ref, wao_ref, wco_ref, wout_ref,
                  o_ref, qbuf, kbuf, vbuf, vwin, cbuf, a_cur, a_prev, c_cur, c_prev, mbuf, fill):
    i = pl.program_id(0)
    tm = q_ref.shape[0]
    nblk = tm // BLOCK
    kvw = N_KV_HEADS * HEAD_DIM

    @pl.when(i == 0)
    def _():
        kbuf[0:BLOCK, :] = jnp.zeros((BLOCK, kvw), BF16)
        vbuf[0:BLOCK, :] = jnp.zeros((BLOCK, kvw), BF16)
        cbuf[0:SUBLANES, :] = jnp.zeros((SUBLANES, cbuf.shape[1]), F32)
        a_prev[...] = jnp.zeros(a_prev.shape, BF16)
        c_prev[...] = jnp.zeros(c_prev.shape, BF16)
        col = lax.broadcasted_iota(jnp.int32, (BLOCK, 2 * BLOCK), 1)
        for h in range(N_Q_HEADS):
            fill[h] = jnp.where(col == 0, sink_ref[h], -jnp.inf)

    @pl.when(i > 0)
    def _():
        kbuf[0:BLOCK, :] = kbuf[tm:tm + BLOCK, :]
        vbuf[0:BLOCK, :] = vbuf[tm:tm + BLOCK, :]
        cbuf[0:SUBLANES, :] = cbuf[tm:tm + SUBLANES, :]


    half = ROT_DIM // 2
    tab = tab_ref[...]
    swapped = pltpu.roll(tab, HEAD_DIM, 1)
    lane = lax.broadcasted_iota(jnp.int32, tab.shape, 1)
    low = lane < HEAD_DIM
    hl = lane & (HEAD_DIM - 1)
    cos = jnp.where(low, tab, swapped)
    ss = jnp.where(low, swapped, tab)
    sa = jnp.where(hl < half, ss, 0.0)
    sb = jnp.where(hl >= half, ss, 0.0)

    def rope(t):
        return t * cos + pltpu.roll(t, LANES - half, 1) * sa + pltpu.roll(t, half, 1) * sb

    scale = HEAD_DIM ** -0.5

    def rope_q(c):
        sl = slice(c * LANES, (c + 1) * LANES)
        qbuf[:, sl] = (rope(q_ref[:, sl].astype(F32)) * scale).astype(BF16)

    def rope_k(c):
        sl = slice(c * LANES, (c + 1) * LANES)
        kbuf[BLOCK:, sl] = rope(kv_ref[:, sl].astype(F32)).astype(BF16)

    def stage_v():
        vbuf[BLOCK:, :] = kv_ref[:, kvw:2 * kvw]
        pack = 2 * SUBLANES
        row = lax.broadcasted_iota(jnp.int32, (pack, kvw), 0)
        for b in range(nblk):
            r0 = b * BLOCK
            head = vbuf[r0:r0 + pack, :].astype(F32)
            vwin[b, 0:pack, :] = jnp.where(row == 0, 0.0, head).astype(BF16)
            vwin[b, pack:, :] = vbuf[r0 + pack:r0 + 2 * BLOCK, :]

    prologue = ([functools.partial(rope_k, c) for c in range(kvw // LANES)] + [stage_v]
                + [functools.partial(rope_q, c) for c in range(q_ref.shape[1] // LANES)])

    qi = lax.broadcasted_iota(jnp.int32, (BLOCK, 2 * BLOCK), 0)
    ci = lax.broadcasted_iota(jnp.int32, (BLOCK, 2 * BLOCK), 1)
    rel = qi + BLOCK - ci
    band = (rel >= 0) & (rel < BLOCK)
    band_first = band & (ci >= jnp.where(i == 0, BLOCK, 0))

    def attn_task(b, h):
        g = h // Q_PER_KV
        rows = slice(b * BLOCK, (b + 1) * BLOCK)
        hcols = slice(h * HEAD_DIM, (h + 1) * HEAD_DIM)
        gcols = slice(g * HEAD_DIM, (g + 1) * HEAD_DIM)
        s = lax.dot_general(qbuf[rows, hcols], kbuf[b * BLOCK:(b + 2) * BLOCK, gcols],
                            (((1,), (1,)), ((), ())), preferred_element_type=F32)
        yield
        s = jnp.where(band_first if b == 0 else band, s, fill[h])
        m = jnp.max(s, axis=-1, keepdims=True)
        p = jnp.exp(s - m)
        denom = jnp.sum(p, axis=-1, keepdims=True)
        w = p.astype(BF16)
        yield
        o = jnp.dot(w, vwin[b, :, gcols], preferred_element_type=F32)
        a_cur[rows, hcols] = (o * (1.0 / denom)).astype(BF16)
        yield

    tasks = [attn_task(b, h) for b in range(nblk) for h in range(N_Q_HEADS)]

    cw = convw_ref[...]
    conv_cols = 2 * LANES

    def conv_chunk(c):
        sl = slice(c * conv_cols, (c + 1) * conv_cols)
        blk, off = divmod(c * conv_cols, Z_BLK)
        zsl = slice(off, off + conv_cols)
        cb_ref, cc_ref, ch_ref = ((cb0_ref, cc0_ref, ch0_ref), (cb1_ref, cc1_ref, ch1_ref))[blk]
        cbuf[SUBLANES:, sl] = cc_ref[:, zsl].astype(F32) * ch_ref[:, zsl].astype(F32)
        y = cw[2:3, sl] * cbuf[SUBLANES:, sl]
        y = y + cw[1:2, sl] * cbuf[SUBLANES - 1:SUBLANES - 1 + tm, sl]
        y = y + cw[0:1, sl] * cbuf[SUBLANES - 2:SUBLANES - 2 + tm, sl]
        c_cur[:, sl] = (cb_ref[:, zsl].astype(F32) * y).astype(BF16)

    conv = [functools.partial(conv_chunk, c) for c in range(cbuf.shape[1] // conv_cols)]

    gates = ((ga0_ref, gc0_ref), (ga1_ref, gc1_ref), (ga2_ref, gc2_ref), (ga3_ref, gc3_ref))

    def merge_chunk(c):
        ga_ref, gc_ref = gates[c]
        ya = jnp.dot(a_prev[...], wao_ref[c], preferred_element_type=F32)
        yc = jnp.dot(c_prev[...], wco_ref[c], preferred_element_type=F32)
        merged = _sigmoid(ga_ref[...].astype(F32)) * ya + _sigmoid(gc_ref[...].astype(F32)) * yc
        mbuf[:, c * Z_BLK:(c + 1) * Z_BLK] = merged.astype(BF16)

    def out_chunk(c):
        sl = slice(c * Z_BLK, (c + 1) * Z_BLK)
        o_ref[:, sl] = x_ref[:, sl] + jnp.dot(mbuf[...], wout_ref[c], preferred_element_type=F32)

    proj = ([functools.partial(merge_chunk, c) for c in range(N_GATE_BLKS)]
            + [functools.partial(out_chunk, c) for c in range(N_GATE_BLKS)])

    proj.pop(0)()
    for thunk in prologue:
        thunk()
    depth = 3
    n_ticks = len(tasks) + depth - 1
    side = proj + conv
    side_at = {(k + 1) * n_ticks // (len(side) + 1): t for k, t in enumerate(side)}
    assert len(side_at) == len(side)
    for t in range(n_ticks):
        for k in range(max(0, t - depth + 1), min(t + 1, len(tasks))):
            next(tasks[k])
        if t in side_at:
            side_at[t]()

    a_prev[...] = a_cur[...]
    c_prev[...] = c_cur[...]


def _const_spec(shape):
    return pl.BlockSpec(shape, lambda i: (0,) * len(shape), pipeline_mode=pl.Buffered(1))


def _mixer(z, x, tab, sinks, conv_w, w_attn_o, w_conv_o, w_out, tm):
    s, d = x.shape
    aw = N_Q_HEADS * HEAD_DIM
    kvw = N_KV_HEADS * HEAD_DIM
    cw = conv_w.shape[1]
    n = s // tm
    assert aw == 2 * Z_BLK and 2 * kvw == Z_BLK and cw == 2 * Z_BLK and d == N_GATE_BLKS * Z_BLK
    o_kv, o_cb, o_cc, o_ch, o_ga, o_gc = 2, 3, 5, 7, 9, 9 + N_GATE_BLKS

    def cur(k):
        return pl.BlockSpec((tm, Z_BLK), lambda i: (jnp.minimum(i, n - 1), k))

    def prev(k):
        return pl.BlockSpec((tm, Z_BLK), lambda i: (jnp.maximum(i - 1, 0), k))

    in_specs = (
        [pl.BlockSpec((tm, aw), lambda i: (jnp.minimum(i, n - 1), 0)), cur(o_kv)]
        + [cur(o_cb), cur(o_cb + 1), cur(o_cc), cur(o_cc + 1), cur(o_ch), cur(o_ch + 1)]
        + [prev(o_ga + k) for k in range(N_GATE_BLKS)]
        + [prev(o_gc + k) for k in range(N_GATE_BLKS)]
        + [pl.BlockSpec((tm, d), lambda i: (jnp.maximum(i - 1, 0), 0)),
           pl.BlockSpec((tm, LANES), lambda i: (jnp.minimum(i, n - 1), 0)),
           pl.BlockSpec(memory_space=pltpu.SMEM),
           _const_spec((CONV_K, cw)), _const_spec(w_attn_o.shape), _const_spec(w_conv_o.shape),
           _const_spec(w_out.shape)]
    )
    return pl.pallas_call(
        _mixer_kernel,
        grid=(n + 1,),
        in_specs=in_specs,
        out_specs=pl.BlockSpec((tm, d), lambda i: (jnp.maximum(i - 1, 0), 0)),
        out_shape=jax.ShapeDtypeStruct((s, d), F32),
        scratch_shapes=[
            pltpu.VMEM((tm, aw), BF16),
            pltpu.VMEM((tm + BLOCK, kvw), BF16),
            pltpu.VMEM((tm + BLOCK, kvw), BF16),
            pltpu.VMEM((tm // BLOCK, 2 * BLOCK, kvw), BF16),
            pltpu.VMEM((tm + SUBLANES, cw), F32),
            pltpu.VMEM((tm, aw), BF16),
            pltpu.VMEM((tm, aw), BF16),
            pltpu.VMEM((tm, cw), BF16),
            pltpu.VMEM((tm, cw), BF16),
            pltpu.VMEM((tm, d), BF16),
            pltpu.VMEM((N_Q_HEADS, BLOCK, 2 * BLOCK), F32),
        ],
        compiler_params=pltpu.CompilerParams(
            dimension_semantics=("arbitrary",), vmem_limit_bytes=VMEM_LIMIT),
        name="mixer",
    )(*([z] * 16), x, tab, sinks, conv_w, w_attn_o, w_conv_o, w_out)


def _xattn_kernel(h_ref, g_ref, wq_ref, kv_ref, wo_ref, o_ref, obuf):
    h = h_ref[...]
    n = _rms(h, g_ref[...]).astype(BF16)
    xw = X_HEADS * X_HEAD_DIM
    q = jnp.dot(n, wq_ref[...], preferred_element_type=F32) * (X_HEAD_DIM ** -0.5)
    q = q.astype(BF16)
    for hd in range(X_HEADS):
        sl = slice(hd * X_HEAD_DIM, (hd + 1) * X_HEAD_DIM)
        k = kv_ref[:, sl]
        v = kv_ref[:, xw + hd * X_HEAD_DIM: xw + (hd + 1) * X_HEAD_DIM]
        s = lax.dot_general(q[:, sl], k, (((1,), (1,)), ((), ())), preferred_element_type=F32)
        m = jnp.max(s, axis=-1, keepdims=True)
        p = jnp.exp(s - m)
        p = (p / jnp.sum(p, axis=-1, keepdims=True)).astype(BF16)
        obuf[:, sl] = jnp.dot(p, v, preferred_element_type=F32).astype(BF16)
    o_ref[...] = h + jnp.dot(obuf[...], wo_ref[...], preferred_element_type=F32)


def _xattn(h, g, w_xq, kv_mem, w_xo, tm):
    s, d = h.shape
    xw = w_xq.shape[1]
    return pl.pallas_call(
        _xattn_kernel,
        grid=(s // tm,),
        in_specs=[
            pl.BlockSpec((tm, d), lambda i: (i, 0)),
            _const_spec((1, d)),
            _const_spec((d, xw)),
            _const_spec(kv_mem.shape),
            _const_spec((xw, d)),
        ],
        out_specs=pl.BlockSpec((tm, d), lambda i: (i, 0)),
        out_shape=jax.ShapeDtypeStruct((s, d), F32),
        scratch_shapes=[pltpu.VMEM((tm, xw), BF16)],
        compiler_params=pltpu.CompilerParams(
            dimension_semantics=("arbitrary",), vmem_limit_bytes=VMEM_LIMIT),
        name="xattn",
    )(h, g.reshape(1, d), w_xq, kv_mem, w_xo)


def _ffn_kernel(h_ref, g_ref, wg_ref, wu_ref, wd_ref, gf_ref, o_ref, n_ref, *, final_norm):
    j = pl.program_id(1)

    @pl.when(j == 0)
    def _():
        h = h_ref[...]
        n_ref[...] = _rms(h, g_ref[...]).astype(BF16)
        o_ref[...] = h

    n = n_ref[...]
    g = jnp.dot(n, wg_ref[...], preferred_element_type=F32)
    u = jnp.dot(n, wu_ref[...], preferred_element_type=F32)
    act = (g * _sigmoid(g) * u).astype(BF16)
    o_ref[...] += jnp.dot(act, wd_ref[...], preferred_element_type=F32)

    if final_norm:
        @pl.when(j == pl.num_programs(1) - 1)
        def _():
            o_ref[...] = _rms(o_ref[...], gf_ref[...])


def _ffn(h, g, w_gate_up, w_down, g_final, tm, tf, final_norm):
    s, d = h.shape
    dff = w_down.shape[0]
    nf = dff // tf
    return pl.pallas_call(
        functools.partial(_ffn_kernel, final_norm=final_norm),
        grid=(s // tm, nf),
        in_specs=[
            pl.BlockSpec((tm, d), lambda i, j: (i, 0)),
            pl.BlockSpec((1, d), lambda i, j: (0, 0)),
            pl.BlockSpec((d, tf), lambda i, j: (0, j)),
            pl.BlockSpec((d, tf), lambda i, j: (0, j + nf)),
            pl.BlockSpec((tf, d), lambda i, j: (j, 0)),
            pl.BlockSpec((1, d), lambda i, j: (0, 0)),
        ],
        out_specs=pl.BlockSpec((tm, d), lambda i, j: (i, 0)),
        out_shape=jax.ShapeDtypeStruct((s, d), F32),
        scratch_shapes=[pltpu.VMEM((tm, d), BF16)],
        compiler_params=pltpu.CompilerParams(
            dimension_semantics=("arbitrary", "arbitrary"), vmem_limit_bytes=VMEM_LIMIT),
        name="ffn",
    )(h, g.reshape(1, d), w_gate_up, w_gate_up, w_down, g_final.reshape(1, d))


def _rope_table(s):
    inv_freq = ROPE_THETA ** (-jnp.arange(0, ROT_DIM, 2, dtype=F32) / ROT_DIM)
    ang = jnp.arange(s, dtype=jnp.int32).astype(F32)[:, None] * inv_freq[None, :]
    cos, sin = jnp.cos(ang), jnp.sin(ang)
    ones = jnp.ones((s, HEAD_DIM - ROT_DIM), F32)
    zeros = jnp.zeros((s, HEAD_DIM - ROT_DIM), F32)
    return jnp.concatenate([cos, cos, ones, -sin, sin, zeros], axis=1)


def _col_blocks(w):
    k, n = w.shape
    return w.astype(BF16).reshape(k, n // Z_BLK, Z_BLK).transpose(1, 0, 2)


def kernel(x, mem, g_mix, w_in, conv_w, sinks, w_attn_o, w_conv_o, w_out, g_xattn, g_mem,
           w_xq, w_xkv, w_xo, g_ffn, w_gate_up, w_down, g_final):
    batch, s, d = x.shape
    depth = w_in.shape[0]
    if depth < 1:
        raise ValueError("depth must be positive")
    tab = _rope_table(s)

    outs = []
    for b in range(batch):
        h = x[b]
        for l in range(depth):
            z = _in_proj(h, g_mix[l], w_in[l].astype(BF16), tm=1024, tn=512)
            h = _mixer(z, h, tab, sinks[l], conv_w[l], _col_blocks(w_attn_o[l]),
                       _col_blocks(w_conv_o[l]), _col_blocks(w_out[l]), tm=256)
            kv_mem = _mem_kv(mem[b], g_mem[l], w_xkv[l].astype(BF16))
            h = _xattn(h, g_xattn[l], w_xq[l].astype(BF16), kv_mem, w_xo[l].astype(BF16), tm=512)
            h = _ffn(h, g_ffn[l], w_gate_up[l].astype(BF16), w_down[l].astype(BF16), g_final,
                     tm=512, tf=512, final_norm=(l == depth - 1))
        outs.append(h)
    return jnp.stack(outs, axis=0)
```

```python
import functools

import jax
import jax.numpy as jnp
import numpy as np
from jax import lax
from jax.experimental import pallas as pl
from jax.experimental.pallas import tpu as pltpu

F32 = jnp.float32
BF16 = jnp.bfloat16

EPS = 1e-6
HEAD_DIM = 64
N_Q_HEADS = 16
N_KV_HEADS = 4
Q_PER_KV = N_Q_HEADS // N_KV_HEADS
BLOCK = 128
ROT_DIM = HEAD_DIM // 4
ROPE_THETA = 500000.0
CONV_K = 3
X_HEADS = 4
X_HEAD_DIM = 128
LANES = 128
SUBLANES = 8

VMEM_LIMIT = 56 * 1024 * 1024


def _rms(x, g):
    ms = jnp.mean(x * x, axis=-1, keepdims=True)
    return x * lax.rsqrt(ms + EPS) * g


def _sigmoid(x):
    return 0.5 * jnp.tanh(0.5 * x) + 0.5


def _mem_kv_kernel(mem_ref, g_ref, w_ref, o_ref):
    n = _rms(mem_ref[...], g_ref[...]).astype(BF16)
    o_ref[...] = jnp.dot(n, w_ref[...], preferred_element_type=F32).astype(o_ref.dtype)


def _mem_kv(mem, g_mem, w_xkv):
    n_mem, d = mem.shape
    return pl.pallas_call(
        _mem_kv_kernel,
        out_shape=jax.ShapeDtypeStruct((n_mem, w_xkv.shape[1]), BF16),
        compiler_params=pltpu.CompilerParams(vmem_limit_bytes=VMEM_LIMIT),
        name="mem_kv",
    )(mem, g_mem.reshape(1, d), w_xkv)


def _in_proj_kernel(x_ref, g_ref, w_ref, o_ref, *, tn):
    n = _rms(x_ref[...], g_ref[...]).astype(BF16)
    for c in range(o_ref.shape[1] // tn):
        sl = slice(c * tn, (c + 1) * tn)
        o_ref[:, sl] = jnp.dot(n, w_ref[:, sl], preferred_element_type=F32).astype(o_ref.dtype)


def _in_proj(x, g, w, tm, tn):
    s, d = x.shape
    n = w.shape[1]
    return pl.pallas_call(
        functools.partial(_in_proj_kernel, tn=tn),
        grid=(s // tm,),
        in_specs=[
            pl.BlockSpec((tm, d), lambda i: (i, 0)),
            _const_spec((1, d)),
            _const_spec((d, n)),
        ],
        out_specs=pl.BlockSpec((tm, n), lambda i: (i, 0)),
        out_shape=jax.ShapeDtypeStruct((s, n), BF16),
        compiler_params=pltpu.CompilerParams(
            dimension_semantics=("arbitrary",), vmem_limit_bytes=VMEM_LIMIT),
        name="in_proj",
    )(x, g.reshape(1, d), w)


Z_BLK = 512
N_GATE_BLKS = 4


def _mixer_kernel(*refs):
    nb = N_GATE_BLKS
    q_ref, kv_ref, cb0_ref, cb1_ref, cc0_ref, cc1_ref, ch0_ref, ch1_ref = refs[:8]
    ga_refs, gc_refs = refs[8:8 + nb], refs[8 + nb:8 + 2 * nb]
    x_ref, tab_ref, sink_ref, convw_ref = refs[8 + 2 * nb:12 + 2 * nb]
    wao_refs, wco_refs, wout_refs = (refs[12 + (2 + k) * nb:12 + (3 + k) * nb] for k in range(3))
    o_ref, qbuf, kbuf, vbuf, vwin, cbuf, a_cur, a_prev, c_cur, c_prev, mbuf, fill = refs[12 + 5 * nb:]
    i = pl.program_id(0)
    tm = q_ref.shape[0]
    nblk = tm // BLOCK
    kvw = N_KV_HEADS * HEAD_DIM

    @pl.when(i == 0)
    def _():
        kbuf[0:BLOCK, :] = jnp.zeros((BLOCK, kvw), BF16)
        vbuf[0:BLOCK, :] = jnp.zeros((BLOCK, kvw), BF16)
        cbuf[0:SUBLANES, :] = jnp.zeros((SUBLANES, cbuf.shape[1]), F32)
        a_prev[...] = jnp.zeros(a_prev.shape, BF16)
        c_prev[...] = jnp.zeros(c_prev.shape, BF16)
        col = lax.broadcasted_iota(jnp.int32, (BLOCK, 2 * BLOCK), 1)
        for h in range(N_Q_HEADS):
            fill[h] = jnp.where(col == 0, sink_ref[h], -jnp.inf)

    @pl.when(i > 0)
    def _():
        kbuf[0:BLOCK, :] = kbuf[tm:tm + BLOCK, :]
        vbuf[0:BLOCK, :] = vbuf[tm:tm + BLOCK, :]
        cbuf[0:SUBLANES, :] = cbuf[tm:tm + SUBLANES, :]


    half = ROT_DIM // 2
    tab = tab_ref[...]
    swapped = pltpu.roll(tab, HEAD_DIM, 1)
    lane = lax.broadcasted_iota(jnp.int32, tab.shape, 1)
    low = lane < HEAD_DIM
    hl = lane & (HEAD_DIM - 1)
    cos = jnp.where(low, tab, swapped)
    ss = jnp.where(low, swapped, tab)
    sa = jnp.where(hl < half, ss, 0.0)
    sb = jnp.where(hl >= half, ss, 0.0)

    def rope(t):
        return t * cos + pltpu.roll(t, LANES - half, 1) * sa + pltpu.roll(t, half, 1) * sb

    scale = HEAD_DIM ** -0.5

    def rope_q(c):
        sl = slice(c * LANES, (c + 1) * LANES)
        qbuf[:, sl] = (rope(q_ref[:, sl].astype(F32)) * scale).astype(BF16)

    def rope_k(c):
        sl = slice(c * LANES, (c + 1) * LANES)
        kbuf[BLOCK:, sl] = rope(kv_ref[:, sl].astype(F32)).astype(BF16)

    def stage_v():
        vbuf[BLOCK:, :] = kv_ref[:, kvw:2 * kvw]
        pack = 2 * SUBLANES
        row = lax.broadcasted_iota(jnp.int32, (pack, kvw), 0)
        for b in range(nblk):
            r0 = b * BLOCK
            head = vbuf[r0:r0 + pack, :].astype(F32)
            vwin[b, 0:pack, :] = jnp.where(row == 0, 0.0, head).astype(BF16)
            vwin[b, pack:, :] = vbuf[r0 + pack:r0 + 2 * BLOCK, :]

    prologue = ([functools.partial(rope_k, c) for c in range(kvw // LANES)] + [stage_v]
                + [functools.partial(rope_q, c) for c in range(q_ref.shape[1] // LANES)])

    qi = lax.broadcasted_iota(jnp.int32, (BLOCK, 2 * BLOCK), 0)
    ci = lax.broadcasted_iota(jnp.int32, (BLOCK, 2 * BLOCK), 1)
    rel = qi + BLOCK - ci
    band = (rel >= 0) & (rel < BLOCK)
    band_first = band & (ci >= jnp.where(i == 0, BLOCK, 0))

    def attn_task(b, h):
        g = h // Q_PER_KV
        rows = slice(b * BLOCK, (b + 1) * BLOCK)
        hcols = slice(h * HEAD_DIM, (h + 1) * HEAD_DIM)
        gcols = slice(g * HEAD_DIM, (g + 1) * HEAD_DIM)
        s = lax.dot_general(qbuf[rows, hcols], kbuf[b * BLOCK:(b + 2) * BLOCK, gcols],
                            (((1,), (1,)), ((), ())), preferred_element_type=F32)
        yield
        s = jnp.where(band_first if b == 0 else band, s, fill[h])
        m = jnp.max(s, axis=-1, keepdims=True)
        p = jnp.exp(s - m)
        denom = jnp.sum(p, axis=-1, keepdims=True)
        w = p.astype(BF16)
        yield
        o = jnp.dot(w, vwin[b, :, gcols], preferred_element_type=F32)
        a_cur[rows, hcols] = (o * (1.0 / denom)).astype(BF16)
        yield

    tasks = [attn_task(b, h) for b in range(nblk) for h in range(N_Q_HEADS)]

    cw = convw_ref[...]
    conv_cols = 2 * LANES

    def conv_chunk(c):
        sl = slice(c * conv_cols, (c + 1) * conv_cols)
        blk, off = divmod(c * conv_cols, Z_BLK)
        zsl = slice(off, off + conv_cols)
        cb_ref, cc_ref, ch_ref = ((cb0_ref, cc0_ref, ch0_ref), (cb1_ref, cc1_ref, ch1_ref))[blk]
        cbuf[SUBLANES:, sl] = cc_ref[:, zsl].astype(F32) * ch_ref[:, zsl].astype(F32)
        y = cw[2:3, sl] * cbuf[SUBLANES:, sl]
        y = y + cw[1:2, sl] * cbuf[SUBLANES - 1:SUBLANES - 1 + tm, sl]
        y = y + cw[0:1, sl] * cbuf[SUBLANES - 2:SUBLANES - 2 + tm, sl]
        c_cur[:, sl] = (cb_ref[:, zsl].astype(F32) * y).astype(BF16)

    conv = [functools.partial(conv_chunk, c) for c in range(cbuf.shape[1] // conv_cols)]

    def merge_chunk(c):
        ya = jnp.dot(a_prev[...], wao_refs[c][...], preferred_element_type=F32)
        yc = jnp.dot(c_prev[...], wco_refs[c][...], preferred_element_type=F32)
        merged = (_sigmoid(ga_refs[c][...].astype(F32)) * ya
                  + _sigmoid(gc_refs[c][...].astype(F32)) * yc)
        mbuf[:, c * Z_BLK:(c + 1) * Z_BLK] = merged.astype(BF16)

    def out_chunk(c):
        sl = slice(c * Z_BLK, (c + 1) * Z_BLK)
        o_ref[:, sl] = x_ref[:, sl] + jnp.dot(mbuf[...], wout_refs[c][...], preferred_element_type=F32)

    proj = ([functools.partial(merge_chunk, c) for c in range(N_GATE_BLKS)]
            + [functools.partial(out_chunk, c) for c in range(N_GATE_BLKS)])

    proj.pop(0)()
    for thunk in prologue:
        thunk()
    depth = 3
    n_ticks = len(tasks) + depth - 1
    side = proj + conv
    side_at = {(k + 1) * n_ticks // (len(side) + 1): t for k, t in enumerate(side)}
    assert len(side_at) == len(side)
    for t in range(n_ticks):
        for k in range(max(0, t - depth + 1), min(t + 1, len(tasks))):
            next(tasks[k])
        if t in side_at:
            side_at[t]()

    a_prev[...] = a_cur[...]
    c_prev[...] = c_cur[...]


def _const_spec(shape):
    return pl.BlockSpec(shape, lambda i: (0,) * len(shape), pipeline_mode=pl.Buffered(1))


def _mixer(z, x, tab, sinks, conv_w, w_attn_o, w_conv_o, w_out, tm):
    s, d = x.shape
    aw = N_Q_HEADS * HEAD_DIM
    kvw = N_KV_HEADS * HEAD_DIM
    cw = conv_w.shape[1]
    n = s // tm
    assert aw == 2 * Z_BLK and 2 * kvw == Z_BLK and cw == 2 * Z_BLK and d == N_GATE_BLKS * Z_BLK
    o_kv, o_cb, o_cc, o_ch, o_ga, o_gc = 2, 3, 5, 7, 9, 9 + N_GATE_BLKS

    def cur(k):
        return pl.BlockSpec((tm, Z_BLK), lambda i: (jnp.minimum(i, n - 1), k))

    def prev(k):
        return pl.BlockSpec((tm, Z_BLK), lambda i: (jnp.maximum(i - 1, 0), k))

    def col_block(w, k):
        return pl.BlockSpec((w.shape[0], Z_BLK), lambda i: (0, k), pipeline_mode=pl.Buffered(1))

    in_specs = (
        [pl.BlockSpec((tm, aw), lambda i: (jnp.minimum(i, n - 1), 0)), cur(o_kv)]
        + [cur(o_cb), cur(o_cb + 1), cur(o_cc), cur(o_cc + 1), cur(o_ch), cur(o_ch + 1)]
        + [prev(o_ga + k) for k in range(N_GATE_BLKS)]
        + [prev(o_gc + k) for k in range(N_GATE_BLKS)]
        + [pl.BlockSpec((tm, d), lambda i: (jnp.maximum(i - 1, 0), 0)),
           pl.BlockSpec((tm, LANES), lambda i: (jnp.minimum(i, n - 1), 0)),
           pl.BlockSpec(memory_space=pltpu.SMEM),
           _const_spec((CONV_K, cw))]
        + [col_block(w, k) for w in (w_attn_o, w_conv_o, w_out) for k in range(N_GATE_BLKS)]
    )
    weights = [w for w in (w_attn_o, w_conv_o, w_out) for _ in range(N_GATE_BLKS)]
    return pl.pallas_call(
        _mixer_kernel,
        grid=(n + 1,),
        in_specs=in_specs,
        out_specs=pl.BlockSpec((tm, d), lambda i: (jnp.maximum(i - 1, 0), 0)),
        out_shape=jax.ShapeDtypeStruct((s, d), F32),
        scratch_shapes=[
            pltpu.VMEM((tm, aw), BF16),
            pltpu.VMEM((tm + BLOCK, kvw), BF16),
            pltpu.VMEM((tm + BLOCK, kvw), BF16),
            pltpu.VMEM((tm // BLOCK, 2 * BLOCK, kvw), BF16),
            pltpu.VMEM((tm + SUBLANES, cw), F32),
            pltpu.VMEM((tm, aw), BF16),
            pltpu.VMEM((tm, aw), BF16),
            pltpu.VMEM((tm, cw), BF16),
            pltpu.VMEM((tm, cw), BF16),
            pltpu.VMEM((tm, d), BF16),
            pltpu.VMEM((N_Q_HEADS, BLOCK, 2 * BLOCK), F32),
        ],
        compiler_params=pltpu.CompilerParams(
            dimension_semantics=("arbitrary",), vmem_limit_bytes=VMEM_LIMIT),
        name="mixer",
    )(*([z] * 16), x, tab, sinks, conv_w, *weights)


def _xattn_kernel(h_ref, g_ref, wq_ref, kv_ref, wo_ref, o_ref, obuf):
    h = h_ref[...]
    n = _rms(h, g_ref[...]).astype(BF16)
    xw = X_HEADS * X_HEAD_DIM
    q = jnp.dot(n, wq_ref[...], preferred_element_type=F32) * (X_HEAD_DIM ** -0.5)
    q = q.astype(BF16)
    for hd in range(X_HEADS):
        sl = slice(hd * X_HEAD_DIM, (hd + 1) * X_HEAD_DIM)
        k = kv_ref[:, sl]
        v = kv_ref[:, xw + hd * X_HEAD_DIM: xw + (hd + 1) * X_HEAD_DIM]
        s = lax.dot_general(q[:, sl], k, (((1,), (1,)), ((), ())), preferred_element_type=F32)
        m = jnp.max(s, axis=-1, keepdims=True)
        p = jnp.exp(s - m)
        p = (p / jnp.sum(p, axis=-1, keepdims=True)).astype(BF16)
        obuf[:, sl] = jnp.dot(p, v, preferred_element_type=F32).astype(BF16)
    o_ref[...] = h + jnp.dot(obuf[...], wo_ref[...], preferred_element_type=F32)


def _xattn(h, g, w_xq, kv_mem, w_xo, tm):
    s, d = h.shape
    xw = w_xq.shape[1]
    return pl.pallas_call(
        _xattn_kernel,
        grid=(s // tm,),
        in_specs=[
            pl.BlockSpec((tm, d), lambda i: (i, 0)),
            _const_spec((1, d)),
            _const_spec((d, xw)),
            _const_spec(kv_mem.shape),
            _const_spec((xw, d)),
        ],
        out_specs=pl.BlockSpec((tm, d), lambda i: (i, 0)),
        out_shape=jax.ShapeDtypeStruct((s, d), F32),
        scratch_shapes=[pltpu.VMEM((tm, xw), BF16)],
        compiler_params=pltpu.CompilerParams(
            dimension_semantics=("arbitrary",), vmem_limit_bytes=VMEM_LIMIT),
        name="xattn",
    )(h, g.reshape(1, d), w_xq, kv_mem, w_xo)


def _ffn_kernel(h_ref, g_ref, wg_ref, wu_ref, wd_ref, gf_ref, o_ref, n_ref, *, final_norm):
    j = pl.program_id(1)

    @pl.when(j == 0)
    def _():
        h = h_ref[...]
        n_ref[...] = _rms(h, g_ref[...]).astype(BF16)
        o_ref[...] = h

    n = n_ref[...]
    g = jnp.dot(n, wg_ref[...], preferred_element_type=F32)
    u = jnp.dot(n, wu_ref[...], preferred_element_type=F32)
    act = (g * _sigmoid(g) * u).astype(BF16)
    o_ref[...] += jnp.dot(act, wd_ref[...], preferred_element_type=F32)

    if final_norm:
        @pl.when(j == pl.num_programs(1) - 1)
        def _():
            o_ref[...] = _rms(o_ref[...], gf_ref[...])


def _ffn(h, g, w_gate_up, w_down, g_final, tm, tf, final_norm):
    s, d = h.shape
    dff = w_down.shape[0]
    nf = dff // tf
    return pl.pallas_call(
        functools.partial(_ffn_kernel, final_norm=final_norm),
        grid=(s // tm, nf),
        in_specs=[
            pl.BlockSpec((tm, d), lambda i, j: (i, 0)),
            pl.BlockSpec((1, d), lambda i, j: (0, 0)),
            pl.BlockSpec((d, tf), lambda i, j: (0, j)),
            pl.BlockSpec((d, tf), lambda i, j: (0, j + nf)),
            pl.BlockSpec((tf, d), lambda i, j: (j, 0)),
            pl.BlockSpec((1, d), lambda i, j: (0, 0)),
        ],
        out_specs=pl.BlockSpec((tm, d), lambda i, j: (i, 0)),
        out_shape=jax.ShapeDtypeStruct((s, d), F32),
        scratch_shapes=[pltpu.VMEM((tm, d), BF16)],
        compiler_params=pltpu.CompilerParams(
            dimension_semantics=("arbitrary", "arbitrary"), vmem_limit_bytes=VMEM_LIMIT),
        name="ffn",
    )(h, g.reshape(1, d), w_gate_up, w_gate_up, w_down, g_final.reshape(1, d))


def _rope_table(s):
    inv_freq = ROPE_THETA ** (-np.arange(0, ROT_DIM, 2, dtype=np.float64) / ROT_DIM)
    ang = np.arange(s, dtype=np.float64)[:, None] * inv_freq[None, :]
    cos, sin = np.cos(ang), np.sin(ang)
    ones = np.ones((s, HEAD_DIM - ROT_DIM))
    zeros = np.zeros((s, HEAD_DIM - ROT_DIM))
    return jnp.asarray(np.concatenate([cos, cos, ones, -sin, sin, zeros], axis=1), dtype=F32)


def kernel(x, mem, g_mix, w_in, conv_w, sinks, w_attn_o, w_conv_o, w_out, g_xattn, g_mem,
           w_xq, w_xkv, w_xo, g_ffn, w_gate_up, w_down, g_final):
    batch, s, d = x.shape
    depth = w_in.shape[0]
    if depth < 1:
        raise ValueError("depth must be positive")
    tab = _rope_table(s)

    outs = []
    for b in range(batch):
        h = x[b]
        for l in range(depth):
            z = _in_proj(h, g_mix[l], w_in[l].astype(BF16), tm=256, tn=512)
            h = _mixer(z, h, tab, sinks[l], conv_w[l], w_attn_o[l].astype(BF16),
                       w_conv_o[l].astype(BF16), w_out[l].astype(BF16), tm=256)
            kv_mem = _mem_kv(mem[b], g_mem[l], w_xkv[l].astype(BF16))
            h = _xattn(h, g_xattn[l], w_xq[l].astype(BF16), kv_mem, w_xo[l].astype(BF16), tm=512)
            h = _ffn(h, g_ffn[l], w_gate_up[l].astype(BF16), w_down[l].astype(BF16), g_final,
                     tm=512, tf=512, final_norm=(l == depth - 1))
        outs.append(h)
    return jnp.stack(outs, axis=0)
```

```python
import functools

import jax
import jax.numpy as jnp
import numpy as np
from jax import lax
from jax.experimental import pallas as pl
from jax.experimental.pallas import tpu as pltpu

F32 = jnp.float32
BF16 = jnp.bfloat16

EPS = 1e-6
HEAD_DIM = 64
N_Q_HEADS = 16
N_KV_HEADS = 4
Q_PER_KV = N_Q_HEADS // N_KV_HEADS
BLOCK = 128
ROT_DIM = HEAD_DIM // 4
ROPE_THETA = 500000.0
CONV_K = 3
X_HEADS = 4
X_HEAD_DIM = 128
LANES = 128
SUBLANES = 8

VMEM_LIMIT = 56 * 1024 * 1024


def _rms(x, g):
    ms = jnp.mean(x * x, axis=-1, keepdims=True)
    return x * lax.rsqrt(ms + EPS) * g


def _sigmoid(x):
    return 0.5 * jnp.tanh(0.5 * x) + 0.5


def _const_spec(shape):
    return pl.BlockSpec(shape, lambda i: (0,) * len(shape), pipeline_mode=pl.Buffered(1))


def _mem_kv_kernel(mem_ref, g_ref, w_ref, o_ref):
    n = _rms(mem_ref[...], g_ref[...]).astype(BF16)
    o_ref[...] = jnp.dot(n, w_ref[...], preferred_element_type=F32).astype(o_ref.dtype)


def _mem_kv(mem, g_mem, w_xkv):
    n_mem, d = mem.shape
    return pl.pallas_call(
        _mem_kv_kernel,
        out_shape=jax.ShapeDtypeStruct((n_mem, w_xkv.shape[1]), BF16),
        compiler_params=pltpu.CompilerParams(vmem_limit_bytes=VMEM_LIMIT),
        name="mem_kv",
    )(mem, g_mem.reshape(1, d), w_xkv)


Z_BLK = 512
CH_Q, CH_KV, CH_CB, CH_CC, CH_CH, CH_GATE, CH_END = 0, 2, 3, 5, 7, 9, 17
N_GATE_BLKS = 4


def _in_mix_kernel(x_ref, g_ref, w_ref, tab_ref, sink_ref, convw_ref,
                   a_ref, cv_ref, zg_ref, qbuf, kbuf, vbuf, vwin, cbuf, cbb, fill):
    i = pl.program_id(0)
    tm = x_ref.shape[0]
    nblk = tm // BLOCK
    kvw = N_KV_HEADS * HEAD_DIM

    @pl.when(i == 0)
    def _():
        kbuf[0:BLOCK, :] = jnp.zeros((BLOCK, kvw), BF16)
        vbuf[0:BLOCK, :] = jnp.zeros((BLOCK, kvw), BF16)
        cbuf[0:SUBLANES, :] = jnp.zeros((SUBLANES, cbuf.shape[1]), F32)
        col = lax.broadcasted_iota(jnp.int32, (BLOCK, 2 * BLOCK), 1)
        for h in range(N_Q_HEADS):
            fill[h] = jnp.where(col == 0, sink_ref[h], -jnp.inf)

    @pl.when(i > 0)
    def _():
        kbuf[0:BLOCK, :] = kbuf[tm:tm + BLOCK, :]
        vbuf[0:BLOCK, :] = vbuf[tm:tm + BLOCK, :]
        cbuf[0:SUBLANES, :] = cbuf[tm:tm + SUBLANES, :]

    n = _rms(x_ref[...], g_ref[...]).astype(BF16)

    def chunk(c):
        return jnp.dot(n, w_ref[:, c * Z_BLK:(c + 1) * Z_BLK], preferred_element_type=F32)

    half = ROT_DIM // 2
    tab = tab_ref[...]
    swapped = pltpu.roll(tab, HEAD_DIM, 1)
    lane = lax.broadcasted_iota(jnp.int32, tab.shape, 1)
    low = lane < HEAD_DIM
    hl = lane & (HEAD_DIM - 1)
    cos = jnp.where(low, tab, swapped)
    ss = jnp.where(low, swapped, tab)
    sa = jnp.where(hl < half, ss, 0.0)
    sb = jnp.where(hl >= half, ss, 0.0)

    def rope(t):
        return t * cos + pltpu.roll(t, LANES - half, 1) * sa + pltpu.roll(t, half, 1) * sb

    scale = HEAD_DIM ** -0.5

    def q_chunk(c):
        r = chunk(c)
        for g in range(Z_BLK // LANES):
            col = (c - CH_Q) * Z_BLK + g * LANES
            qbuf[:, col:col + LANES] = (rope(r[:, g * LANES:(g + 1) * LANES]) * scale).astype(BF16)

    def kv_chunk():
        r = chunk(CH_KV)
        for g in range(kvw // LANES):
            sl = slice(g * LANES, (g + 1) * LANES)
            kbuf[BLOCK:, sl] = rope(r[:, sl]).astype(BF16)
        vbuf[BLOCK:, :] = r[:, kvw:2 * kvw].astype(BF16)
        pack = 2 * SUBLANES
        row = lax.broadcasted_iota(jnp.int32, (pack, kvw), 0)
        for b in range(nblk):
            r0 = b * BLOCK
            head = vbuf[r0:r0 + pack, :].astype(F32)
            vwin[b, 0:pack, :] = jnp.where(row == 0, 0.0, head).astype(BF16)
            vwin[b, pack:, :] = vbuf[r0 + pack:r0 + 2 * BLOCK, :]

    def cb_chunk(c):
        cbb[:, (c - CH_CB) * Z_BLK:(c - CH_CB + 1) * Z_BLK] = chunk(c)

    def cc_chunk(c):
        cbuf[SUBLANES:, (c - CH_CC) * Z_BLK:(c - CH_CC + 1) * Z_BLK] = chunk(c)

    def ch_chunk(c):
        sl = slice((c - CH_CH) * Z_BLK, (c - CH_CH + 1) * Z_BLK)
        cbuf[SUBLANES:, sl] = cbuf[SUBLANES:, sl] * chunk(c)

    def gate_chunk(c):
        zg_ref[:, (c - CH_GATE) * Z_BLK:(c - CH_GATE + 1) * Z_BLK] = chunk(c).astype(BF16)

    cw = convw_ref[...]
    conv_cols = 2 * LANES

    def conv_chunk(c):
        sl = slice(c * conv_cols, (c + 1) * conv_cols)
        y = cw[2:3, sl] * cbuf[SUBLANES:, sl]
        y = y + cw[1:2, sl] * cbuf[SUBLANES - 1:SUBLANES - 1 + tm, sl]
        y = y + cw[0:1, sl] * cbuf[SUBLANES - 2:SUBLANES - 2 + tm, sl]
        cv_ref[:, sl] = (cbb[:, sl] * y).astype(BF16)

    qi = lax.broadcasted_iota(jnp.int32, (BLOCK, 2 * BLOCK), 0)
    ci = lax.broadcasted_iota(jnp.int32, (BLOCK, 2 * BLOCK), 1)
    rel = qi + BLOCK - ci
    band = (rel >= 0) & (rel < BLOCK)
    band_first = band & (ci >= jnp.where(i == 0, BLOCK, 0))

    def attn_task(b, h):
        g = h // Q_PER_KV
        rows = slice(b * BLOCK, (b + 1) * BLOCK)
        hcols = slice(h * HEAD_DIM, (h + 1) * HEAD_DIM)
        gcols = slice(g * HEAD_DIM, (g + 1) * HEAD_DIM)
        s = lax.dot_general(qbuf[rows, hcols], kbuf[b * BLOCK:(b + 2) * BLOCK, gcols],
                            (((1,), (1,)), ((), ())), preferred_element_type=F32)
        yield
        s = jnp.where(band_first if b == 0 else band, s, fill[h])
        m = jnp.max(s, axis=-1, keepdims=True)
        p = jnp.exp(s - m)
        denom = jnp.sum(p, axis=-1, keepdims=True)
        w = p.astype(BF16)
        yield
        o = jnp.dot(w, vwin[b, :, gcols], preferred_element_type=F32)
        a_ref[rows, hcols] = (o * (1.0 / denom)).astype(BF16)
        yield

    tasks = [attn_task(b, h) for b in range(nblk) for h in range(N_Q_HEADS)]
    depth = 3
    n_ticks = len(tasks) + depth - 1
    ticks = iter(range(n_ticks))

    def run_ticks(count):
        for _ in range(count):
            t = next(ticks)
            for k in range(max(0, t - depth + 1), min(t + 1, len(tasks))):
                next(tasks[k])

    for c in range(CH_Q, CH_KV):
        q_chunk(c)
    kv_chunk()
    rest = ([functools.partial(cb_chunk, c) for c in range(CH_CB, CH_CC)]
            + [functools.partial(cc_chunk, c) for c in range(CH_CC, CH_CH)]
            + [functools.partial(ch_chunk, c) for c in range(CH_CH, CH_GATE)]
            + [functools.partial(gate_chunk, c) for c in range(CH_GATE, CH_END)])
    n_conv = cbuf.shape[1] // conv_cols
    for k, thunk in enumerate(rest):
        thunk()
        run_ticks((k + 1) * n_ticks // len(rest) - k * n_ticks // len(rest))
        kc = k - (CH_GATE - CH_CB)
        if 0 <= kc < n_conv:
            conv_chunk(kc)
    assert next(ticks, None) is None


def _in_mix(x, g, w, tab, sinks, conv_w, tm):
    s, d = x.shape
    n = w.shape[1]
    aw = N_Q_HEADS * HEAD_DIM
    kvw = N_KV_HEADS * HEAD_DIM
    cw = conv_w.shape[1]
    gw = n - CH_GATE * Z_BLK
    assert aw == (CH_KV - CH_Q) * Z_BLK and 2 * kvw == Z_BLK and cw == (CH_CC - CH_CB) * Z_BLK
    assert n == CH_END * Z_BLK and gw == 2 * d
    return pl.pallas_call(
        _in_mix_kernel,
        grid=(s // tm,),
        in_specs=[
            pl.BlockSpec((tm, d), lambda i: (i, 0)),
            _const_spec((1, d)),
            _const_spec((d, n)),
            pl.BlockSpec((tm, LANES), lambda i: (i, 0)),
            pl.BlockSpec(memory_space=pltpu.SMEM),
            _const_spec((CONV_K, cw)),
        ],
        out_specs=[
            pl.BlockSpec((tm, aw), lambda i: (i, 0)),
            pl.BlockSpec((tm, cw), lambda i: (i, 0)),
            pl.BlockSpec((tm, gw), lambda i: (i, 0)),
        ],
        out_shape=[
            jax.ShapeDtypeStruct((s, aw), BF16),
            jax.ShapeDtypeStruct((s, cw), BF16),
            jax.ShapeDtypeStruct((s, gw), BF16),
        ],
        scratch_shapes=[
            pltpu.VMEM((tm, aw), BF16),
            pltpu.VMEM((tm + BLOCK, kvw), BF16),
            pltpu.VMEM((tm + BLOCK, kvw), BF16),
            pltpu.VMEM((tm // BLOCK, 2 * BLOCK, kvw), BF16),
            pltpu.VMEM((tm + SUBLANES, cw), F32),
            pltpu.VMEM((tm, cw), F32),
            pltpu.VMEM((N_Q_HEADS, BLOCK, 2 * BLOCK), F32),
        ],
        compiler_params=pltpu.CompilerParams(
            dimension_semantics=("arbitrary",), vmem_limit_bytes=VMEM_LIMIT),
        name="in_mix",
    )(x, g.reshape(1, d), w, tab, sinks, conv_w)


def _proj_kernel(*refs):
    nb = N_GATE_BLKS
    a_ref, cv_ref, zg_ref, x_ref = refs[:4]
    wao_refs, wco_refs, wout_refs = (refs[4 + k * nb:4 + (k + 1) * nb] for k in range(3))
    o_ref, mbuf = refs[4 + 3 * nb:]
    d = x_ref.shape[1]
    for c in range(nb):
        sl = slice(c * Z_BLK, (c + 1) * Z_BLK)
        ya = jnp.dot(a_ref[...], wao_refs[c][...], preferred_element_type=F32)
        yc = jnp.dot(cv_ref[...], wco_refs[c][...], preferred_element_type=F32)
        ga = zg_ref[:, sl].astype(F32)
        gc = zg_ref[:, d + c * Z_BLK:d + (c + 1) * Z_BLK].astype(F32)
        mbuf[:, sl] = (_sigmoid(ga) * ya + _sigmoid(gc) * yc).astype(BF16)
    for c in range(nb):
        sl = slice(c * Z_BLK, (c + 1) * Z_BLK)
        o_ref[:, sl] = x_ref[:, sl] + jnp.dot(mbuf[...], wout_refs[c][...], preferred_element_type=F32)


def _proj(a, cv, zg, x, w_attn_o, w_conv_o, w_out, tm):
    s, d = x.shape
    assert d == N_GATE_BLKS * Z_BLK

    def rows(width):
        return pl.BlockSpec((tm, width), lambda i: (i, 0))

    def col_block(w, k):
        return pl.BlockSpec((w.shape[0], Z_BLK), lambda i: (0, k), pipeline_mode=pl.Buffered(1))

    ws = (w_attn_o, w_conv_o, w_out)
    return pl.pallas_call(
        _proj_kernel,
        grid=(s // tm,),
        in_specs=[rows(a.shape[1]), rows(cv.shape[1]), rows(zg.shape[1]), rows(d)]
        + [col_block(w, k) for w in ws for k in range(N_GATE_BLKS)],
        out_specs=rows(d),
        out_shape=jax.ShapeDtypeStruct((s, d), F32),
        scratch_shapes=[pltpu.VMEM((tm, d), BF16)],
        compiler_params=pltpu.CompilerParams(
            dimension_semantics=("arbitrary",), vmem_limit_bytes=VMEM_LIMIT),
        name="proj",
    )(a, cv, zg, x, *[w for w in ws for _ in range(N_GATE_BLKS)])


def _xattn_kernel(h_ref, g_ref, wq_ref, kv_ref, wo_ref, o_ref, obuf):
    h = h_ref[...]
    n = _rms(h, g_ref[...]).astype(BF16)
    xw = X_HEADS * X_HEAD_DIM
    q = jnp.dot(n, wq_ref[...], preferred_element_type=F32) * (X_HEAD_DIM ** -0.5)
    q = q.astype(BF16)
    for hd in range(X_HEADS):
        sl = slice(hd * X_HEAD_DIM, (hd + 1) * X_HEAD_DIM)
        k = kv_ref[:, sl]
        v = kv_ref[:, xw + hd * X_HEAD_DIM: xw + (hd + 1) * X_HEAD_DIM]
        s = lax.dot_general(q[:, sl], k, (((1,), (1,)), ((), ())), preferred_element_type=F32)
        m = jnp.max(s, axis=-1, keepdims=True)
        p = jnp.exp(s - m)
        p = (p / jnp.sum(p, axis=-1, keepdims=True)).astype(BF16)
        obuf[:, sl] = jnp.dot(p, v, preferred_element_type=F32).astype(BF16)
    o_ref[...] = h + jnp.dot(obuf[...], wo_ref[...], preferred_element_type=F32)


def _xattn(h, g, w_xq, kv_mem, w_xo, tm):
    s, d = h.shape
    xw = w_xq.shape[1]
    return pl.pallas_call(
        _xattn_kernel,
        grid=(s // tm,),
        in_specs=[
            pl.BlockSpec((tm, d), lambda i: (i, 0)),
            _const_spec((1, d)),
            _const_spec((d, xw)),
            _const_spec(kv_mem.shape),
            _const_spec((xw, d)),
        ],
        out_specs=pl.BlockSpec((tm, d), lambda i: (i, 0)),
        out_shape=jax.ShapeDtypeStruct((s, d), F32),
        scratch_shapes=[pltpu.VMEM((tm, xw), BF16)],
        compiler_params=pltpu.CompilerParams(
            dimension_semantics=("arbitrary",), vmem_limit_bytes=VMEM_LIMIT),
        name="xattn",
    )(h, g.reshape(1, d), w_xq, kv_mem, w_xo)


def _ffn_kernel(h_ref, g_ref, wg_ref, wu_ref, wd_ref, gf_ref, o_ref, n_ref, *, final_norm):
    j = pl.program_id(1)

    @pl.when(j == 0)
    def _():
        h = h_ref[...]
        n_ref[...] = _rms(h, g_ref[...]).astype(BF16)
        o_ref[...] = h

    n = n_ref[...]
    g = jnp.dot(n, wg_ref[...], preferred_element_type=F32)
    u = jnp.dot(n, wu_ref[...], preferred_element_type=F32)
    act = (g * _sigmoid(g) * u).astype(BF16)
    o_ref[...] += jnp.dot(act, wd_ref[...], preferred_element_type=F32)

    if final_norm:
        @pl.when(j == pl.num_programs(1) - 1)
        def _():
            o_ref[...] = _rms(o_ref[...], gf_ref[...])


def _ffn(h, g, w_gate_up, w_down, g_final, tm, tf, final_norm):
    s, d = h.shape
    dff = w_down.shape[0]
    nf = dff // tf
    return pl.pallas_call(
        functools.partial(_ffn_kernel, final_norm=final_norm),
        grid=(s // tm, nf),
        in_specs=[
            pl.BlockSpec((tm, d), lambda i, j: (i, 0)),
            pl.BlockSpec((1, d), lambda i, j: (0, 0)),
            pl.BlockSpec((d, tf), lambda i, j: (0, j)),
            pl.BlockSpec((d, tf), lambda i, j: (0, j + nf)),
            pl.BlockSpec((tf, d), lambda i, j: (j, 0)),
            pl.BlockSpec((1, d), lambda i, j: (0, 0)),
        ],
        out_specs=pl.BlockSpec((tm, d), lambda i, j: (i, 0)),
        out_shape=jax.ShapeDtypeStruct((s, d), F32),
        scratch_shapes=[pltpu.VMEM((tm, d), BF16)],
        compiler_params=pltpu.CompilerParams(
            dimension_semantics=("arbitrary", "arbitrary"), vmem_limit_bytes=VMEM_LIMIT),
        name="ffn",
    )(h, g.reshape(1, d), w_gate_up, w_gate_up, w_down, g_final.reshape(1, d))


def _rope_table(s):
    inv_freq = ROPE_THETA ** (-np.arange(0, ROT_DIM, 2, dtype=np.float64) / ROT_DIM)
    ang = np.arange(s, dtype=np.float64)[:, None] * inv_freq[None, :]
    cos, sin = np.cos(ang), np.sin(ang)
    ones = np.ones((s, HEAD_DIM - ROT_DIM))
    zeros = np.zeros((s, HEAD_DIM - ROT_DIM))
    return jnp.asarray(np.concatenate([cos, cos, ones, -sin, sin, zeros], axis=1), dtype=F32)


def kernel(x, mem, g_mix, w_in, conv_w, sinks, w_attn_o, w_conv_o, w_out, g_xattn, g_mem,
           w_xq, w_xkv, w_xo, g_ffn, w_gate_up, w_down, g_final):
    batch, s, d = x.shape
    depth = w_in.shape[0]
    if depth < 1:
        raise ValueError("depth must be positive")
    tab = _rope_table(s)

    outs = []
    for b in range(batch):
        h = x[b]
        for l in range(depth):
            a, cv, zg = _in_mix(h, g_mix[l], w_in[l].astype(BF16), tab, sinks[l], conv_w[l], tm=256)
            h = _proj(a, cv, zg, h, w_attn_o[l].astype(BF16), w_conv_o[l].astype(BF16),
                      w_out[l].astype(BF16), tm=512)
            kv_mem = _mem_kv(mem[b], g_mem[l], w_xkv[l].astype(BF16))
            h = _xattn(h, g_xattn[l], w_xq[l].astype(BF16), kv_mem, w_xo[l].astype(BF16), tm=512)
            h = _ffn(h, g_ffn[l], w_gate_up[l].astype(BF16), w_down[l].astype(BF16), g_final,
                     tm=512, tf=512, final_norm=(l == depth - 1))
        outs.append(h)
    return jnp.stack(outs, axis=0)
```

```python
import functools

import jax
import jax.numpy as jnp
import numpy as np
from jax import lax
from jax.experimental import pallas as pl
from jax.experimental.pallas import tpu as pltpu

F32 = jnp.float32
BF16 = jnp.bfloat16

EPS = 1e-6
HEAD_DIM = 64
N_Q_HEADS = 16
N_KV_HEADS = 4
Q_PER_KV = N_Q_HEADS // N_KV_HEADS
BLOCK = 128
ROT_DIM = HEAD_DIM // 4
ROPE_THETA = 500000.0
CONV_K = 3
X_HEADS = 4
X_HEAD_DIM = 128
LANES = 128
SUBLANES = 8

VMEM_LIMIT = 56 * 1024 * 1024


def _rms(x, g):
    ms = jnp.mean(x * x, axis=-1, keepdims=True)
    return x * lax.rsqrt(ms + EPS) * g


def _sigmoid(x):
    return 0.5 * jnp.tanh(0.5 * x) + 0.5


def _const_spec(shape):
    return pl.BlockSpec(shape, lambda i: (0,) * len(shape), pipeline_mode=pl.Buffered(1))


def _mem_kv_kernel(mem_ref, g_ref, w_ref, o_ref):
    n = _rms(mem_ref[...], g_ref[...]).astype(BF16)
    o_ref[...] = jnp.dot(n, w_ref[...], preferred_element_type=F32).astype(o_ref.dtype)


def _mem_kv(mem, g_mem, w_xkv):
    n_mem, d = mem.shape
    return pl.pallas_call(
        _mem_kv_kernel,
        out_shape=jax.ShapeDtypeStruct((n_mem, w_xkv.shape[1]), BF16),
        compiler_params=pltpu.CompilerParams(vmem_limit_bytes=VMEM_LIMIT),
        name="mem_kv",
    )(mem, g_mem.reshape(1, d), w_xkv)


Z_BLK = 512
CH_Q, CH_KV, CH_CB, CH_CC, CH_CH, CH_GATE, CH_END = 0, 2, 3, 5, 7, 9, 17
N_GATE_BLKS = 4


def _in_mix_kernel(x_ref, g_ref, w_ref, tab_ref, sink_ref, convw_ref,
                   a_ref, cv_ref, zg_ref, qbuf, kbuf, vbuf, vwin, cbuf, cbb, fill):
    i = pl.program_id(0)
    tm = x_ref.shape[0]
    nblk = tm // BLOCK
    kvw = N_KV_HEADS * HEAD_DIM

    @pl.when(i == 0)
    def _():
        kbuf[0:BLOCK, :] = jnp.zeros((BLOCK, kvw), BF16)
        vbuf[0:BLOCK, :] = jnp.zeros((BLOCK, kvw), BF16)
        cbuf[0:SUBLANES, :] = jnp.zeros((SUBLANES, cbuf.shape[1]), F32)
        col = lax.broadcasted_iota(jnp.int32, (BLOCK, 2 * BLOCK), 1)
        for h in range(N_Q_HEADS):
            fill[h] = jnp.where(col == 0, sink_ref[h], -jnp.inf)

    @pl.when(i > 0)
    def _():
        kbuf[0:BLOCK, :] = kbuf[tm:tm + BLOCK, :]
        vbuf[0:BLOCK, :] = vbuf[tm:tm + BLOCK, :]
        cbuf[0:SUBLANES, :] = cbuf[tm:tm + SUBLANES, :]

    n = _rms(x_ref[...], g_ref[...]).astype(BF16)

    def chunk(c):
        return jnp.dot(n, w_ref[:, c * Z_BLK:(c + 1) * Z_BLK], preferred_element_type=F32)

    half = ROT_DIM // 2
    tab = tab_ref[...]
    swapped = pltpu.roll(tab, HEAD_DIM, 1)
    lane = lax.broadcasted_iota(jnp.int32, tab.shape, 1)
    low = lane < HEAD_DIM
    hl = lane & (HEAD_DIM - 1)
    cos = jnp.where(low, tab, swapped)
    ss = jnp.where(low, swapped, tab)
    sa = jnp.where(hl < half, ss, 0.0)
    sb = jnp.where(hl >= half, ss, 0.0)

    def rope(t):
        return t * cos + pltpu.roll(t, LANES - half, 1) * sa + pltpu.roll(t, half, 1) * sb

    scale = HEAD_DIM ** -0.5

    def q_chunk(c):
        r = chunk(c)
        for g in range(Z_BLK // LANES):
            col = (c - CH_Q) * Z_BLK + g * LANES
            qbuf[:, col:col + LANES] = (rope(r[:, g * LANES:(g + 1) * LANES]) * scale).astype(BF16)

    def kv_chunk():
        r = chunk(CH_KV)
        for g in range(kvw // LANES):
            sl = slice(g * LANES, (g + 1) * LANES)
            kbuf[BLOCK:, sl] = rope(r[:, sl]).astype(BF16)
        vbuf[BLOCK:, :] = r[:, kvw:2 * kvw].astype(BF16)
        pack = 2 * SUBLANES
        row = lax.broadcasted_iota(jnp.int32, (pack, kvw), 0)
        for b in range(nblk):
            r0 = b * BLOCK
            head = vbuf[r0:r0 + pack, :].astype(F32)
            vwin[b, 0:pack, :] = jnp.where(row == 0, 0.0, head).astype(BF16)
            vwin[b, pack:, :] = vbuf[r0 + pack:r0 + 2 * BLOCK, :]

    def cb_chunk(c):
        cbb[:, (c - CH_CB) * Z_BLK:(c - CH_CB + 1) * Z_BLK] = chunk(c)

    def cc_chunk(c):
        cbuf[SUBLANES:, (c - CH_CC) * Z_BLK:(c - CH_CC + 1) * Z_BLK] = chunk(c)

    def ch_chunk(c):
        sl = slice((c - CH_CH) * Z_BLK, (c - CH_CH + 1) * Z_BLK)
        cbuf[SUBLANES:, sl] = cbuf[SUBLANES:, sl] * chunk(c)

    def gate_chunk(c):
        zg_ref[:, (c - CH_GATE) * Z_BLK:(c - CH_GATE + 1) * Z_BLK] = chunk(c).astype(BF16)

    cw = convw_ref[...]
    conv_cols = 2 * LANES

    def conv_chunk(c):
        sl = slice(c * conv_cols, (c + 1) * conv_cols)
        y = cw[2:3, sl] * cbuf[SUBLANES:, sl]
        y = y + cw[1:2, sl] * cbuf[SUBLANES - 1:SUBLANES - 1 + tm, sl]
        y = y + cw[0:1, sl] * cbuf[SUBLANES - 2:SUBLANES - 2 + tm, sl]
        cv_ref[:, sl] = (cbb[:, sl] * y).astype(BF16)

    qi = lax.broadcasted_iota(jnp.int32, (BLOCK, 2 * BLOCK), 0)
    ci = lax.broadcasted_iota(jnp.int32, (BLOCK, 2 * BLOCK), 1)
    rel = qi + BLOCK - ci
    band = (rel >= 0) & (rel < BLOCK)
    band_first = band & (ci >= jnp.where(i == 0, BLOCK, 0))

    def attn_task(b, h):
        g = h // Q_PER_KV
        rows = slice(b * BLOCK, (b + 1) * BLOCK)
        hcols = slice(h * HEAD_DIM, (h + 1) * HEAD_DIM)
        gcols = slice(g * HEAD_DIM, (g + 1) * HEAD_DIM)
        s = lax.dot_general(qbuf[rows, hcols], kbuf[b * BLOCK:(b + 2) * BLOCK, gcols],
                            (((1,), (1,)), ((), ())), preferred_element_type=F32)
        yield
        s = jnp.where(band_first if b == 0 else band, s, fill[h])
        m = jnp.max(s, axis=-1, keepdims=True)
        p = jnp.exp(s - m)
        denom = jnp.sum(p, axis=-1, keepdims=True)
        w = p.astype(BF16)
        yield
        o = jnp.dot(w, vwin[b, :, gcols], preferred_element_type=F32)
        a_ref[rows, hcols] = (o * (1.0 / denom)).astype(BF16)
        yield

    tasks = [attn_task(b, h) for b in range(nblk) for h in range(N_Q_HEADS)]
    depth = 3
    n_ticks = len(tasks) + depth - 1
    ticks = iter(range(n_ticks))

    def run_ticks(count):
        for _ in range(count):
            t = next(ticks)
            for k in range(max(0, t - depth + 1), min(t + 1, len(tasks))):
                next(tasks[k])

    for c in range(CH_Q, CH_KV):
        q_chunk(c)
    kv_chunk()
    rest = ([functools.partial(cb_chunk, c) for c in range(CH_CB, CH_CC)]
            + [functools.partial(cc_chunk, c) for c in range(CH_CC, CH_CH)]
            + [functools.partial(ch_chunk, c) for c in range(CH_CH, CH_GATE)]
            + [functools.partial(gate_chunk, c) for c in range(CH_GATE, CH_END)])
    n_conv = cbuf.shape[1] // conv_cols
    for k, thunk in enumerate(rest):
        thunk()
        run_ticks((k + 1) * n_ticks // len(rest) - k * n_ticks // len(rest))
        kc = k - (CH_GATE - CH_CB)
        if 0 <= kc < n_conv:
            conv_chunk(kc)
    assert next(ticks, None) is None


def _in_mix(x, g, w, tab, sinks, conv_w, tm):
    s, d = x.shape
    n = w.shape[1]
    aw = N_Q_HEADS * HEAD_DIM
    kvw = N_KV_HEADS * HEAD_DIM
    cw = conv_w.shape[1]
    gw = n - CH_GATE * Z_BLK
    assert aw == (CH_KV - CH_Q) * Z_BLK and 2 * kvw == Z_BLK and cw == (CH_CC - CH_CB) * Z_BLK
    assert n == CH_END * Z_BLK and gw == 2 * d
    return pl.pallas_call(
        _in_mix_kernel,
        grid=(s // tm,),
        in_specs=[
            pl.BlockSpec((tm, d), lambda i: (i, 0)),
            _const_spec((1, d)),
            _const_spec((d, n)),
            pl.BlockSpec((tm, LANES), lambda i: (i, 0)),
            pl.BlockSpec(memory_space=pltpu.SMEM),
            _const_spec((CONV_K, cw)),
        ],
        out_specs=[
            pl.BlockSpec((tm, aw), lambda i: (i, 0)),
            pl.BlockSpec((tm, cw), lambda i: (i, 0)),
            pl.BlockSpec((tm, gw), lambda i: (i, 0)),
        ],
        out_shape=[
            jax.ShapeDtypeStruct((s, aw), BF16),
            jax.ShapeDtypeStruct((s, cw), BF16),
            jax.ShapeDtypeStruct((s, gw), BF16),
        ],
        scratch_shapes=[
            pltpu.VMEM((tm, aw), BF16),
            pltpu.VMEM((tm + BLOCK, kvw), BF16),
            pltpu.VMEM((tm + BLOCK, kvw), BF16),
            pltpu.VMEM((tm // BLOCK, 2 * BLOCK, kvw), BF16),
            pltpu.VMEM((tm + SUBLANES, cw), F32),
            pltpu.VMEM((tm, cw), F32),
            pltpu.VMEM((N_Q_HEADS, BLOCK, 2 * BLOCK), F32),
        ],
        compiler_params=pltpu.CompilerParams(
            dimension_semantics=("arbitrary",), vmem_limit_bytes=VMEM_LIMIT),
        name="in_mix",
    )(x, g.reshape(1, d), w, tab, sinks, conv_w)


FFN_CAST_ROWS_GU = 32
FFN_CAST_ROWS_D = 176


def _mix_out_kernel(*refs):
    nb = N_GATE_BLKS
    a_ref, cv_ref, zg_ref, x_ref = refs[:4]
    wao_refs, wco_refs, wout_refs = (refs[4 + k * nb:4 + (k + 1) * nb] for k in range(3))
    gx_ref, wxq_ref, kvm_ref = refs[4 + 3 * nb:7 + 3 * nb]
    wxo_refs = refs[7 + 3 * nb:7 + 4 * nb]
    wgu_in, wd_in, o_ref, wgu_out, wd_out, mbuf, h1_cur, h1_prev, obuf = refs[7 + 4 * nb:]
    d = x_ref.shape[1]
    xw = X_HEADS * X_HEAD_DIM

    @pl.when(pl.program_id(0) == 0)
    def _():
        h1_prev[...] = jnp.zeros(h1_prev.shape, F32)

    def merge_chunk(c):
        sl = slice(c * Z_BLK, (c + 1) * Z_BLK)
        ya = jnp.dot(a_ref[...], wao_refs[c][...], preferred_element_type=F32)
        yc = jnp.dot(cv_ref[...], wco_refs[c][...], preferred_element_type=F32)
        ga = zg_ref[:, sl].astype(F32)
        gc = zg_ref[:, d + c * Z_BLK:d + (c + 1) * Z_BLK].astype(F32)
        mbuf[:, sl] = (_sigmoid(ga) * ya + _sigmoid(gc) * yc).astype(BF16)

    def out_chunk(c):
        sl = slice(c * Z_BLK, (c + 1) * Z_BLK)
        h1_cur[:, sl] = x_ref[:, sl] + jnp.dot(mbuf[...], wout_refs[c][...], preferred_element_type=F32)

    state = {}

    def xq():
        nrm = _rms(h1_prev[...], gx_ref[...]).astype(BF16)
        q = jnp.dot(nrm, wxq_ref[...], preferred_element_type=F32) * (X_HEAD_DIM ** -0.5)
        state["q"] = q.astype(BF16)

    def xhead(hd):
        sl = slice(hd * X_HEAD_DIM, (hd + 1) * X_HEAD_DIM)
        s = lax.dot_general(state["q"][:, sl], kvm_ref[:, sl], (((1,), (1,)), ((), ())),
                            preferred_element_type=F32)
        p = jnp.exp(s - jnp.max(s, axis=-1, keepdims=True))
        denom = jnp.sum(p, axis=-1, keepdims=True)
        o = jnp.dot(p.astype(BF16), kvm_ref[:, xw + hd * X_HEAD_DIM:xw + (hd + 1) * X_HEAD_DIM],
                    preferred_element_type=F32)
        obuf[:, sl] = (o * (1.0 / denom)).astype(BF16)

    def xout(c):
        sl = slice(c * Z_BLK, (c + 1) * Z_BLK)
        o_ref[:, sl] = h1_prev[:, sl] + jnp.dot(obuf[...], wxo_refs[c][...], preferred_element_type=F32)

    def cast_gu():
        wgu_out[...] = wgu_in[...].astype(BF16)

    def cast_d():
        wd_out[...] = wd_in[...].astype(BF16)

    part = functools.partial
    for thunk in (part(merge_chunk, 0), xq, cast_gu, part(merge_chunk, 1), part(xhead, 0),
                  part(merge_chunk, 2), part(xhead, 1), part(merge_chunk, 3), part(xhead, 2),
                  part(out_chunk, 0), part(xhead, 3), part(out_chunk, 1), part(xout, 0), cast_d,
                  part(out_chunk, 2), part(xout, 1), part(out_chunk, 3), part(xout, 2), part(xout, 3)):
        thunk()

    h1_prev[...] = h1_cur[...]


def _mix_out(a, cv, zg, x, w_attn_o, w_conv_o, w_out, g_xattn, w_xq, kv_mem, w_xo, w_gate_up, w_down, tm):
    s, d = x.shape
    n = s // tm
    xw = w_xq.shape[1]
    assert d == N_GATE_BLKS * Z_BLK
    assert w_gate_up.shape[0] == n * FFN_CAST_ROWS_GU and 2 * w_down.shape[0] == n * FFN_CAST_ROWS_D

    def cur(width):
        return pl.BlockSpec((tm, width), lambda i: (jnp.minimum(i, n - 1), 0))

    def col_block(w, k):
        return pl.BlockSpec((w.shape[0], Z_BLK), lambda i: (0, k), pipeline_mode=pl.Buffered(1))

    gu_spec = pl.BlockSpec((FFN_CAST_ROWS_GU, w_gate_up.shape[1]), lambda i: (jnp.minimum(i, n - 1), 0))
    d_spec = pl.BlockSpec((FFN_CAST_ROWS_D, w_down.shape[1]), lambda i: (jnp.minimum(i, n - 1) // 2, 0))
    ws = (w_attn_o, w_conv_o, w_out)
    return pl.pallas_call(
        _mix_out_kernel,
        grid=(n + 1,),
        in_specs=[cur(a.shape[1]), cur(cv.shape[1]), cur(zg.shape[1]), cur(d)]
        + [col_block(w, k) for w in ws for k in range(N_GATE_BLKS)]
        + [_const_spec((1, d)), _const_spec((d, xw)), _const_spec(kv_mem.shape)]
        + [col_block(w_xo, k) for k in range(N_GATE_BLKS)]
        + [gu_spec, d_spec],
        out_specs=[pl.BlockSpec((tm, d), lambda i: (jnp.maximum(i - 1, 0), 0)), gu_spec, d_spec],
        out_shape=[
            jax.ShapeDtypeStruct((s, d), F32),
            jax.ShapeDtypeStruct(w_gate_up.shape, BF16),
            jax.ShapeDtypeStruct(w_down.shape, BF16),
        ],
        scratch_shapes=[
            pltpu.VMEM((tm, d), BF16),
            pltpu.VMEM((tm, d), F32),
            pltpu.VMEM((tm, d), F32),
            pltpu.VMEM((tm, xw), BF16),
        ],
        compiler_params=pltpu.CompilerParams(
            dimension_semantics=("arbitrary",), vmem_limit_bytes=VMEM_LIMIT),
        name="mix_out",
    )(a, cv, zg, x, *[w for w in ws for _ in range(N_GATE_BLKS)],
      g_xattn.reshape(1, d), w_xq, kv_mem, *([w_xo] * N_GATE_BLKS), w_gate_up, w_down)


def _ffn_kernel(h_ref, g_ref, wg_ref, wu_ref, wd_ref, gf_ref, o_ref, n_ref, *, final_norm):
    j = pl.program_id(1)

    @pl.when(j == 0)
    def _():
        h = h_ref[...]
        n_ref[...] = _rms(h, g_ref[...]).astype(BF16)
        o_ref[...] = h

    n = n_ref[...]
    g = jnp.dot(n, wg_ref[...], preferred_element_type=F32)
    u = jnp.dot(n, wu_ref[...], preferred_element_type=F32)
    act = (g * _sigmoid(g) * u).astype(BF16)
    o_ref[...] += jnp.dot(act, wd_ref[...], preferred_element_type=F32)

    if final_norm:
        @pl.when(j == pl.num_programs(1) - 1)
        def _():
            o_ref[...] = _rms(o_ref[...], gf_ref[...])


def _ffn(h, g, w_gate_up, w_down, g_final, tm, tf, final_norm):
    s, d = h.shape
    dff = w_down.shape[0]
    nf = dff // tf
    return pl.pallas_call(
        functools.partial(_ffn_kernel, final_norm=final_norm),
        grid=(s // tm, nf),
        in_specs=[
            pl.BlockSpec((tm, d), lambda i, j: (i, 0)),
            pl.BlockSpec((1, d), lambda i, j: (0, 0)),
            pl.BlockSpec((d, tf), lambda i, j: (0, j)),
            pl.BlockSpec((d, tf), lambda i, j: (0, j + nf)),
            pl.BlockSpec((tf, d), lambda i, j: (j, 0)),
            pl.BlockSpec((1, d), lambda i, j: (0, 0)),
        ],
        out_specs=pl.BlockSpec((tm, d), lambda i, j: (i, 0)),
        out_shape=jax.ShapeDtypeStruct((s, d), F32),
        scratch_shapes=[pltpu.VMEM((tm, d), BF16)],
        compiler_params=pltpu.CompilerParams(
            dimension_semantics=("arbitrary", "arbitrary"), vmem_limit_bytes=VMEM_LIMIT),
        name="ffn",
    )(h, g.reshape(1, d), w_gate_up, w_gate_up, w_down, g_final.reshape(1, d))


def _rope_table(s):
    inv_freq = ROPE_THETA ** (-np.arange(0, ROT_DIM, 2, dtype=np.float64) / ROT_DIM)
    ang = np.arange(s, dtype=np.float64)[:, None] * inv_freq[None, :]
    cos, sin = np.cos(ang), np.sin(ang)
    ones = np.ones((s, HEAD_DIM - ROT_DIM))
    zeros = np.zeros((s, HEAD_DIM - ROT_DIM))
    return jnp.asarray(np.concatenate([cos, cos, ones, -sin, sin, zeros], axis=1), dtype=F32)


def kernel(x, mem, g_mix, w_in, conv_w, sinks, w_attn_o, w_conv_o, w_out, g_xattn, g_mem,
           w_xq, w_xkv, w_xo, g_ffn, w_gate_up, w_down, g_final):
    batch, s, d = x.shape
    depth = w_in.shape[0]
    if depth < 1:
        raise ValueError("depth must be positive")
    tab = _rope_table(s)

    outs = []
    for b in range(batch):
        h = x[b]
        for l in range(depth):
            a, cv, zg = _in_mix(h, g_mix[l], w_in[l].astype(BF16), tab, sinks[l], conv_w[l], tm=256)
            kv_mem = _mem_kv(mem[b], g_mem[l], w_xkv[l].astype(BF16))
            h, wgu, wdn = _mix_out(a, cv, zg, h, w_attn_o[l].astype(BF16), w_conv_o[l].astype(BF16),
                                   w_out[l].astype(BF16), g_xattn[l], w_xq[l].astype(BF16), kv_mem,
                                   w_xo[l].astype(BF16), w_gate_up[l], w_down[l], tm=256)
            h = _ffn(h, g_ffn[l], wgu, wdn, g_final, tm=512, tf=512, final_norm=(l == depth - 1))
        outs.append(h)
    return jnp.stack(outs, axis=0)
```

```python
import functools

import jax
import jax.numpy as jnp
import numpy as np
from jax import lax
from jax.experimental import pallas as pl
from jax.experimental.pallas import tpu as pltpu

F32 = jnp.float32
BF16 = jnp.bfloat16

EPS = 1e-6
HEAD_DIM = 64
N_Q_HEADS = 16
N_KV_HEADS = 4
Q_PER_KV = N_Q_HEADS // N_KV_HEADS
BLOCK = 128
ROT_DIM = HEAD_DIM // 4
ROPE_THETA = 500000.0
CONV_K = 3
X_HEADS = 4
X_HEAD_DIM = 128
LANES = 128
SUBLANES = 8

VMEM_LIMIT = 56 * 1024 * 1024


def _rms(x, g):
    ms = jnp.mean(x * x, axis=-1, keepdims=True)
    return x * lax.rsqrt(ms + EPS) * g


def _sigmoid(x):
    return 0.5 * jnp.tanh(0.5 * x) + 0.5


def _const_spec(shape):
    return pl.BlockSpec(shape, lambda i: (0,) * len(shape), pipeline_mode=pl.Buffered(1))


def _mem_kv_kernel(mem_ref, g_ref, w_ref, o_ref):
    n = _rms(mem_ref[...], g_ref[...]).astype(BF16)
    o_ref[...] = jnp.dot(n, w_ref[...], preferred_element_type=F32).astype(o_ref.dtype)


def _mem_kv(mem, g_mem, w_xkv):
    n_mem, d = mem.shape
    return pl.pallas_call(
        _mem_kv_kernel,
        out_shape=jax.ShapeDtypeStruct((n_mem, w_xkv.shape[1]), BF16),
        compiler_params=pltpu.CompilerParams(vmem_limit_bytes=VMEM_LIMIT),
        name="mem_kv",
    )(mem, g_mem.reshape(1, d), w_xkv)


Z_BLK = 512
CH_Q, CH_KV, CH_CB, CH_CC, CH_CH, CH_GATE, CH_END = 0, 2, 3, 5, 7, 9, 17
N_GATE_BLKS = 4


def _in_mix_kernel(x_ref, g_ref, w_ref, tab_ref, sink_ref, convw_ref,
                   a_ref, cv_ref, zg_ref, qbuf, kbuf, vbuf, vwin, cbuf, cbb, fill):
    i = pl.program_id(0)
    tm = x_ref.shape[0]
    nblk = tm // BLOCK
    kvw = N_KV_HEADS * HEAD_DIM

    @pl.when(i == 0)
    def _():
        kbuf[0:BLOCK, :] = jnp.zeros((BLOCK, kvw), BF16)
        vbuf[0:BLOCK, :] = jnp.zeros((BLOCK, kvw), BF16)
        cbuf[0:SUBLANES, :] = jnp.zeros((SUBLANES, cbuf.shape[1]), F32)
        col = lax.broadcasted_iota(jnp.int32, (BLOCK, 2 * BLOCK), 1)
        for h in range(N_Q_HEADS):
            fill[h] = jnp.where(col == 0, sink_ref[h], -jnp.inf)

    @pl.when(i > 0)
    def _():
        kbuf[0:BLOCK, :] = kbuf[tm:tm + BLOCK, :]
        vbuf[0:BLOCK, :] = vbuf[tm:tm + BLOCK, :]
        cbuf[0:SUBLANES, :] = cbuf[tm:tm + SUBLANES, :]

    n = _rms(x_ref[...], g_ref[...]).astype(BF16)

    def chunk(c):
        return jnp.dot(n, w_ref[:, c * Z_BLK:(c + 1) * Z_BLK], preferred_element_type=F32)

    half = ROT_DIM // 2
    tab = tab_ref[...]
    swapped = pltpu.roll(tab, HEAD_DIM, 1)
    lane = lax.broadcasted_iota(jnp.int32, tab.shape, 1)
    low = lane < HEAD_DIM
    hl = lane & (HEAD_DIM - 1)
    cos = jnp.where(low, tab, swapped)
    ss = jnp.where(low, swapped, tab)
    sa = jnp.where(hl < half, ss, 0.0)
    sb = jnp.where(hl >= half, ss, 0.0)

    def rope(t):
        return t * cos + pltpu.roll(t, LANES - half, 1) * sa + pltpu.roll(t, half, 1) * sb

    scale = HEAD_DIM ** -0.5

    def q_chunk(c):
        r = chunk(c)
        for g in range(Z_BLK // LANES):
            col = (c - CH_Q) * Z_BLK + g * LANES
            qbuf[:, col:col + LANES] = (rope(r[:, g * LANES:(g + 1) * LANES]) * scale).astype(BF16)

    def kv_chunk():
        r = chunk(CH_KV)
        for g in range(kvw // LANES):
            sl = slice(g * LANES, (g + 1) * LANES)
            kbuf[BLOCK:, sl] = rope(r[:, sl]).astype(BF16)
        vbuf[BLOCK:, :] = r[:, kvw:2 * kvw].astype(BF16)
        pack = 2 * SUBLANES
        row = lax.broadcasted_iota(jnp.int32, (pack, kvw), 0)
        for b in range(nblk):
            r0 = b * BLOCK
            head = vbuf[r0:r0 + pack, :].astype(F32)
            vwin[b, 0:pack, :] = jnp.where(row == 0, 0.0, head).astype(BF16)
            vwin[b, pack:, :] = vbuf[r0 + pack:r0 + 2 * BLOCK, :]

    def cb_chunk(c):
        cbb[:, (c - CH_CB) * Z_BLK:(c - CH_CB + 1) * Z_BLK] = chunk(c)

    def cc_chunk(c):
        cbuf[SUBLANES:, (c - CH_CC) * Z_BLK:(c - CH_CC + 1) * Z_BLK] = chunk(c)

    def ch_chunk(c):
        sl = slice((c - CH_CH) * Z_BLK, (c - CH_CH + 1) * Z_BLK)
        cbuf[SUBLANES:, sl] = cbuf[SUBLANES:, sl] * chunk(c)

    def gate_chunk(c):
        zg_ref[:, (c - CH_GATE) * Z_BLK:(c - CH_GATE + 1) * Z_BLK] = chunk(c).astype(BF16)

    cw = convw_ref[...]
    conv_cols = 2 * LANES

    def conv_chunk(c):
        sl = slice(c * conv_cols, (c + 1) * conv_cols)
        y = cw[2:3, sl] * cbuf[SUBLANES:, sl]
        y = y + cw[1:2, sl] * cbuf[SUBLANES - 1:SUBLANES - 1 + tm, sl]
        y = y + cw[0:1, sl] * cbuf[SUBLANES - 2:SUBLANES - 2 + tm, sl]
        cv_ref[:, sl] = (cbb[:, sl] * y).astype(BF16)

    qi = lax.broadcasted_iota(jnp.int32, (BLOCK, 2 * BLOCK), 0)
    ci = lax.broadcasted_iota(jnp.int32, (BLOCK, 2 * BLOCK), 1)
    rel = qi + BLOCK - ci
    band = (rel >= 0) & (rel < BLOCK)
    band_first = band & (ci >= jnp.where(i == 0, BLOCK, 0))

    def attn_task(b, h):
        g = h // Q_PER_KV
        rows = slice(b * BLOCK, (b + 1) * BLOCK)
        hcols = slice(h * HEAD_DIM, (h + 1) * HEAD_DIM)
        gcols = slice(g * HEAD_DIM, (g + 1) * HEAD_DIM)
        s = lax.dot_general(qbuf[rows, hcols], kbuf[b * BLOCK:(b + 2) * BLOCK, gcols],
                            (((1,), (1,)), ((), ())), preferred_element_type=F32)
        yield
        s = jnp.where(band_first if b == 0 else band, s, fill[h])
        m = jnp.max(s, axis=-1, keepdims=True)
        p = jnp.exp(s - m)
        denom = jnp.sum(p, axis=-1, keepdims=True)
        w = p.astype(BF16)
        yield
        o = jnp.dot(w, vwin[b, :, gcols], preferred_element_type=F32)
        a_ref[rows, hcols] = (o * (1.0 / denom)).astype(BF16)
        yield

    tasks = [attn_task(b, h) for b in range(nblk) for h in range(N_Q_HEADS)]
    depth = 3
    n_ticks = len(tasks) + depth - 1
    ticks = iter(range(n_ticks))

    def run_ticks(count):
        for _ in range(count):
            t = next(ticks)
            for k in range(max(0, t - depth + 1), min(t + 1, len(tasks))):
                next(tasks[k])

    for c in range(CH_Q, CH_KV):
        q_chunk(c)
    kv_chunk()
    rest = ([functools.partial(cb_chunk, c) for c in range(CH_CB, CH_CC)]
            + [functools.partial(cc_chunk, c) for c in range(CH_CC, CH_CH)]
            + [functools.partial(ch_chunk, c) for c in range(CH_CH, CH_GATE)]
            + [functools.partial(gate_chunk, c) for c in range(CH_GATE, CH_END)])
    n_conv = cbuf.shape[1] // conv_cols
    span = len(rest) - 3
    for k, thunk in enumerate(rest):
        thunk()
        if k < span:
            run_ticks((k + 1) * n_ticks // span - k * n_ticks // span)
        kc = k - (CH_GATE - CH_CB)
        if 0 <= kc < n_conv:
            conv_chunk(kc)
    assert next(ticks, None) is None


def _in_mix(x, g, w, tab, sinks, conv_w, tm):
    s, d = x.shape
    n = w.shape[1]
    aw = N_Q_HEADS * HEAD_DIM
    kvw = N_KV_HEADS * HEAD_DIM
    cw = conv_w.shape[1]
    gw = n - CH_GATE * Z_BLK
    assert aw == (CH_KV - CH_Q) * Z_BLK and 2 * kvw == Z_BLK and cw == (CH_CC - CH_CB) * Z_BLK
    assert n == CH_END * Z_BLK and gw == 2 * d
    return pl.pallas_call(
        _in_mix_kernel,
        grid=(s // tm,),
        in_specs=[
            pl.BlockSpec((tm, d), lambda i: (i, 0)),
            _const_spec((1, d)),
            _const_spec((d, n)),
            pl.BlockSpec((tm, LANES), lambda i: (i, 0)),
            pl.BlockSpec(memory_space=pltpu.SMEM),
            _const_spec((CONV_K, cw)),
        ],
        out_specs=[
            pl.BlockSpec((tm, aw), lambda i: (i, 0)),
            pl.BlockSpec((tm, cw), lambda i: (i, 0)),
            pl.BlockSpec((tm, gw), lambda i: (i, 0)),
        ],
        out_shape=[
            jax.ShapeDtypeStruct((s, aw), BF16),
            jax.ShapeDtypeStruct((s, cw), BF16),
            jax.ShapeDtypeStruct((s, gw), BF16),
        ],
        scratch_shapes=[
            pltpu.VMEM((tm, aw), BF16),
            pltpu.VMEM((tm + BLOCK, kvw), BF16),
            pltpu.VMEM((tm + BLOCK, kvw), BF16),
            pltpu.VMEM((tm // BLOCK, 2 * BLOCK, kvw), BF16),
            pltpu.VMEM((tm + SUBLANES, cw), F32),
            pltpu.VMEM((tm, cw), F32),
            pltpu.VMEM((N_Q_HEADS, BLOCK, 2 * BLOCK), F32),
        ],
        compiler_params=pltpu.CompilerParams(
            dimension_semantics=("arbitrary",), vmem_limit_bytes=VMEM_LIMIT),
        name="in_mix",
    )(x, g.reshape(1, d), w, tab, sinks, conv_w)


FFN_CAST_ROWS_GU = 32
FFN_CAST_ROWS_D = 176


def _mix_out_kernel(*refs):
    nb = N_GATE_BLKS
    a_ref, cv_ref, zg_ref, x_ref = refs[:4]
    wao_refs, wco_refs, wout_refs = (refs[4 + k * nb:4 + (k + 1) * nb] for k in range(3))
    gx_ref, wxq_ref, kvm_ref = refs[4 + 3 * nb:7 + 3 * nb]
    wxo_refs = refs[7 + 3 * nb:7 + 4 * nb]
    wgu_in, wd_in, o_ref, wgu_out, wd_out, mbuf, h1buf, obuf = refs[7 + 4 * nb:]
    d = x_ref.shape[1]
    xw = X_HEADS * X_HEAD_DIM
    i = pl.program_id(0)
    h1_cur = h1buf.at[i % 2]
    h1_prev = h1buf.at[(i + 1) % 2]

    @pl.when(i == 0)
    def _():
        h1buf[1] = jnp.zeros(h1buf.shape[1:], F32)

    def merge_chunk(c):
        sl = slice(c * Z_BLK, (c + 1) * Z_BLK)
        ya = jnp.dot(a_ref[...], wao_refs[c][...], preferred_element_type=F32)
        yc = jnp.dot(cv_ref[...], wco_refs[c][...], preferred_element_type=F32)
        ga = zg_ref[:, sl].astype(F32)
        gc = zg_ref[:, d + c * Z_BLK:d + (c + 1) * Z_BLK].astype(F32)
        mbuf[:, sl] = (_sigmoid(ga) * ya + _sigmoid(gc) * yc).astype(BF16)

    def out_chunk(c):
        sl = slice(c * Z_BLK, (c + 1) * Z_BLK)
        h1_cur[:, sl] = x_ref[:, sl] + jnp.dot(mbuf[...], wout_refs[c][...], preferred_element_type=F32)

    state = {}

    def xq():
        nrm = _rms(h1_prev[...], gx_ref[...]).astype(BF16)
        q = jnp.dot(nrm, wxq_ref[...], preferred_element_type=F32) * (X_HEAD_DIM ** -0.5)
        state["q"] = q.astype(BF16)

    def xhead(hd):
        sl = slice(hd * X_HEAD_DIM, (hd + 1) * X_HEAD_DIM)
        s = lax.dot_general(state["q"][:, sl], kvm_ref[:, sl], (((1,), (1,)), ((), ())),
                            preferred_element_type=F32)
        p = jnp.exp(s - jnp.max(s, axis=-1, keepdims=True))
        denom = jnp.sum(p, axis=-1, keepdims=True)
        o = jnp.dot(p.astype(BF16), kvm_ref[:, xw + hd * X_HEAD_DIM:xw + (hd + 1) * X_HEAD_DIM],
                    preferred_element_type=F32)
        obuf[:, sl] = (o * (1.0 / denom)).astype(BF16)

    def xout(c):
        sl = slice(c * Z_BLK, (c + 1) * Z_BLK)
        o_ref[:, sl] = h1_prev[:, sl] + jnp.dot(obuf[...], wxo_refs[c][...], preferred_element_type=F32)

    def cast_gu():
        wgu_out[...] = wgu_in[...].astype(BF16)

    def cast_d():
        wd_out[...] = wd_in[...].astype(BF16)

    part = functools.partial
    for thunk in (part(merge_chunk, 0), xq, cast_gu, part(merge_chunk, 1), part(xhead, 0),
                  part(merge_chunk, 2), part(xhead, 1), part(merge_chunk, 3), part(xhead, 2),
                  part(out_chunk, 0), part(xhead, 3), part(out_chunk, 1), part(xout, 0), cast_d,
                  part(out_chunk, 2), part(xout, 1), part(out_chunk, 3), part(xout, 2), part(xout, 3)):
        thunk()


def _mix_out(a, cv, zg, x, w_attn_o, w_conv_o, w_out, g_xattn, w_xq, kv_mem, w_xo, w_gate_up, w_down, tm):
    s, d = x.shape
    n = s // tm
    xw = w_xq.shape[1]
    assert d == N_GATE_BLKS * Z_BLK
    assert w_gate_up.shape[0] == n * FFN_CAST_ROWS_GU and 2 * w_down.shape[0] == n * FFN_CAST_ROWS_D

    def cur(width):
        return pl.BlockSpec((tm, width), lambda i: (jnp.minimum(i, n - 1), 0))

    def col_block(w, k):
        return pl.BlockSpec((w.shape[0], Z_BLK), lambda i: (0, k), pipeline_mode=pl.Buffered(1))

    gu_spec = pl.BlockSpec((FFN_CAST_ROWS_GU, w_gate_up.shape[1]), lambda i: (jnp.minimum(i, n - 1), 0))
    d_spec = pl.BlockSpec((FFN_CAST_ROWS_D, w_down.shape[1]), lambda i: (jnp.minimum(i, n - 1) // 2, 0))
    ws = (w_attn_o, w_conv_o, w_out)
    return pl.pallas_call(
        _mix_out_kernel,
        grid=(n + 1,),
        in_specs=[cur(a.shape[1]), cur(cv.shape[1]), cur(zg.shape[1]), cur(d)]
        + [col_block(w, k) for w in ws for k in range(N_GATE_BLKS)]
        + [_const_spec((1, d)), _const_spec((d, xw)), _const_spec(kv_mem.shape)]
        + [col_block(w_xo, k) for k in range(N_GATE_BLKS)]
        + [gu_spec, d_spec],
        out_specs=[pl.BlockSpec((tm, d), lambda i: (jnp.maximum(i - 1, 0), 0)), gu_spec, d_spec],
        out_shape=[
            jax.ShapeDtypeStruct((s, d), F32),
            jax.ShapeDtypeStruct(w_gate_up.shape, BF16),
            jax.ShapeDtypeStruct(w_down.shape, BF16),
        ],
        scratch_shapes=[
            pltpu.VMEM((tm, d), BF16),
            pltpu.VMEM((2, tm, d), F32),
            pltpu.VMEM((tm, xw), BF16),
        ],
        compiler_params=pltpu.CompilerParams(
            dimension_semantics=("arbitrary",), vmem_limit_bytes=VMEM_LIMIT),
        name="mix_out",
    )(a, cv, zg, x, *[w for w in ws for _ in range(N_GATE_BLKS)],
      g_xattn.reshape(1, d), w_xq, kv_mem, *([w_xo] * N_GATE_BLKS), w_gate_up, w_down)


def _ffn_kernel(*refs, final_norm):
    nb = N_GATE_BLKS
    h_ref, g_ref, wg_ref, wu_ref = refs[:4]
    wd_refs = refs[4:4 + nb]
    gf_ref, o_ref, n_ref = refs[4 + nb:]
    j = pl.program_id(1)

    @pl.when(j == 0)
    def _():
        h = h_ref[...]
        n_ref[...] = _rms(h, g_ref[...]).astype(BF16)
        o_ref[...] = h

    n = n_ref[...]
    g = jnp.dot(n, wg_ref[...], preferred_element_type=F32)
    u = jnp.dot(n, wu_ref[...], preferred_element_type=F32)
    act = (g * _sigmoid(g) * u).astype(BF16)
    for c in range(nb):
        sl = slice(c * Z_BLK, (c + 1) * Z_BLK)
        o_ref[:, sl] += jnp.dot(act, wd_refs[c][...], preferred_element_type=F32)

    if final_norm:
        @pl.when(j == pl.num_programs(1) - 1)
        def _():
            o_ref[...] = _rms(o_ref[...], gf_ref[...])


def _ffn(h, g, w_gate_up, w_down, g_final, tm, tf, final_norm):
    s, d = h.shape
    dff = w_down.shape[0]
    nf = dff // tf
    return pl.pallas_call(
        functools.partial(_ffn_kernel, final_norm=final_norm),
        grid=(s // tm, nf),
        in_specs=[
            pl.BlockSpec((tm, d), lambda i, j: (i, 0)),
            pl.BlockSpec((1, d), lambda i, j: (0, 0)),
            pl.BlockSpec((d, tf), lambda i, j: (0, j)),
            pl.BlockSpec((d, tf), lambda i, j: (0, j + nf)),
        ]
        + [pl.BlockSpec((tf, Z_BLK), functools.partial(lambda i, j, c: (j, c), c=c))
           for c in range(N_GATE_BLKS)]
        + [pl.BlockSpec((1, d), lambda i, j: (0, 0))],
        out_specs=pl.BlockSpec((tm, d), lambda i, j: (i, 0)),
        out_shape=jax.ShapeDtypeStruct((s, d), F32),
        scratch_shapes=[pltpu.VMEM((tm, d), BF16)],
        compiler_params=pltpu.CompilerParams(
            dimension_semantics=("arbitrary", "arbitrary"), vmem_limit_bytes=VMEM_LIMIT),
        name="ffn",
    )(h, g.reshape(1, d), w_gate_up, w_gate_up, *([w_down] * N_GATE_BLKS), g_final.reshape(1, d))


def _rope_table(s):
    inv_freq = ROPE_THETA ** (-np.arange(0, ROT_DIM, 2, dtype=np.float64) / ROT_DIM)
    ang = np.arange(s, dtype=np.float64)[:, None] * inv_freq[None, :]
    cos, sin = np.cos(ang), np.sin(ang)
    ones = np.ones((s, HEAD_DIM - ROT_DIM))
    zeros = np.zeros((s, HEAD_DIM - ROT_DIM))
    return jnp.asarray(np.concatenate([cos, cos, ones, -sin, sin, zeros], axis=1), dtype=F32)


def kernel(x, mem, g_mix, w_in, conv_w, sinks, w_attn_o, w_conv_o, w_out, g_xattn, g_mem,
           w_xq, w_xkv, w_xo, g_ffn, w_gate_up, w_down, g_final):
    batch, s, d = x.shape
    depth = w_in.shape[0]
    if depth < 1:
        raise ValueError("depth must be positive")
    tab = _rope_table(s)

    outs = []
    for b in range(batch):
        h = x[b]
        for l in range(depth):
            a, cv, zg = _in_mix(h, g_mix[l], w_in[l].astype(BF16), tab, sinks[l], conv_w[l], tm=256)
            kv_mem = _mem_kv(mem[b], g_mem[l], w_xkv[l].astype(BF16))
            h, wgu, wdn = _mix_out(a, cv, zg, h, w_attn_o[l].astype(BF16), w_conv_o[l].astype(BF16),
                                   w_out[l].astype(BF16), g_xattn[l], w_xq[l].astype(BF16), kv_mem,
                                   w_xo[l].astype(BF16), w_gate_up[l], w_down[l], tm=256)
            h = _ffn(h, g_ffn[l], wgu, wdn, g_final, tm=1024, tf=512, final_norm=(l == depth - 1))
        outs.append(h)
    return jnp.stack(outs, axis=0)
```

```python
import functools

import jax
import jax.numpy as jnp
import numpy as np
from jax import lax
from jax.experimental import pallas as pl
from jax.experimental.pallas import tpu as pltpu

F32 = jnp.float32
BF16 = jnp.bfloat16

EPS = 1e-6
HEAD_DIM = 64
N_Q_HEADS = 16
N_KV_HEADS = 4
Q_PER_KV = N_Q_HEADS // N_KV_HEADS
BLOCK = 128
ROT_DIM = HEAD_DIM // 4
ROPE_THETA = 500000.0
CONV_K = 3
X_HEADS = 4
X_HEAD_DIM = 128
LANES = 128
SUBLANES = 8

VMEM_LIMIT = 56 * 1024 * 1024


def _rms(x, g):
    ms = jnp.mean(x * x, axis=-1, keepdims=True)
    return x * lax.rsqrt(ms + EPS) * g


def _sigmoid(x):
    return 0.5 * jnp.tanh(0.5 * x) + 0.5


def _const_spec(shape):
    return pl.BlockSpec(shape, lambda i: (0,) * len(shape), pipeline_mode=pl.Buffered(1))


def _mem_kv_kernel(mem_ref, g_ref, w_ref, o_ref):
    n = _rms(mem_ref[...], g_ref[...]).astype(BF16)
    o_ref[...] = jnp.dot(n, w_ref[...].astype(BF16), preferred_element_type=F32).astype(o_ref.dtype)


def _mem_kv(mem, g_mem, w_xkv):
    n_mem, d = mem.shape
    return pl.pallas_call(
        _mem_kv_kernel,
        out_shape=jax.ShapeDtypeStruct((n_mem, w_xkv.shape[1]), BF16),
        compiler_params=pltpu.CompilerParams(vmem_limit_bytes=VMEM_LIMIT),
        name="mem_kv",
    )(mem, g_mem.reshape(1, d), w_xkv)


Z_BLK = 512
CH_Q, CH_KV, CH_CB, CH_CC, CH_CH, CH_GATE, CH_END = 0, 2, 3, 5, 7, 9, 17
N_GATE_BLKS = 4


def _in_mix_kernel(*refs, n_cast):
    x_ref, g_ref, w_ref, tab_ref, sink_ref, convw_ref = refs[:6]
    cast_in = refs[6:6 + n_cast]
    a_ref, cv_ref, zg_ref = refs[6 + n_cast:9 + n_cast]
    cast_out = refs[9 + n_cast:9 + 2 * n_cast]
    qbuf, kbuf, vbuf, vwin, cbuf, cbb, fill = refs[9 + 2 * n_cast:]
    i = pl.program_id(0)
    tm = x_ref.shape[0]
    nblk = tm // BLOCK
    kvw = N_KV_HEADS * HEAD_DIM

    @pl.when(i == 0)
    def _():
        kbuf[0:BLOCK, :] = jnp.zeros((BLOCK, kvw), BF16)
        vbuf[0:BLOCK, :] = jnp.zeros((BLOCK, kvw), BF16)
        cbuf[0:SUBLANES, :] = jnp.zeros((SUBLANES, cbuf.shape[1]), F32)
        col = lax.broadcasted_iota(jnp.int32, (SUBLANES, 2 * BLOCK), 1)
        for h in range(N_Q_HEADS):
            fill[h] = jnp.where(col == 0, sink_ref[h], -jnp.inf)

    @pl.when(i > 0)
    def _():
        kbuf[0:BLOCK, :] = kbuf[tm:tm + BLOCK, :]
        vbuf[0:BLOCK, :] = vbuf[tm:tm + BLOCK, :]
        cbuf[0:SUBLANES, :] = cbuf[tm:tm + SUBLANES, :]

    n = _rms(x_ref[...], g_ref[...]).astype(BF16)

    def chunk(c):
        return jnp.dot(n, w_ref[:, c * Z_BLK:(c + 1) * Z_BLK], preferred_element_type=F32)

    half = ROT_DIM // 2
    tab = tab_ref[...]
    swapped = pltpu.roll(tab, HEAD_DIM, 1)
    lane = lax.broadcasted_iota(jnp.int32, tab.shape, 1)
    low = lane < HEAD_DIM
    hl = lane & (HEAD_DIM - 1)
    cos = jnp.where(low, tab, swapped)
    ss = jnp.where(low, swapped, tab)
    sa = jnp.where(hl < half, ss, 0.0)
    sb = jnp.where(hl >= half, ss, 0.0)

    def rope(t):
        return t * cos + pltpu.roll(t, LANES - half, 1) * sa + pltpu.roll(t, half, 1) * sb

    scale = HEAD_DIM ** -0.5

    def q_chunk(c):
        r = chunk(c)
        for g in range(Z_BLK // LANES):
            col = (c - CH_Q) * Z_BLK + g * LANES
            qbuf[:, col:col + LANES] = (rope(r[:, g * LANES:(g + 1) * LANES]) * scale).astype(BF16)

    def kv_chunk():
        r = chunk(CH_KV)
        for g in range(kvw // LANES):
            sl = slice(g * LANES, (g + 1) * LANES)
            kbuf[BLOCK:, sl] = rope(r[:, sl]).astype(BF16)
        vbuf[BLOCK:, :] = r[:, kvw:2 * kvw].astype(BF16)
        pack = 2 * SUBLANES
        row = lax.broadcasted_iota(jnp.int32, (pack, kvw), 0)
        for b in range(nblk):
            r0 = b * BLOCK
            head = vbuf[r0:r0 + pack, :].astype(F32)
            vwin[b, 0:pack, :] = jnp.where(row == 0, 0.0, head).astype(BF16)
            vwin[b, pack:, :] = vbuf[r0 + pack:r0 + 2 * BLOCK, :]

    def cb_chunk(c):
        cbb[:, (c - CH_CB) * Z_BLK:(c - CH_CB + 1) * Z_BLK] = chunk(c)

    def cc_chunk(c):
        cbuf[SUBLANES:, (c - CH_CC) * Z_BLK:(c - CH_CC + 1) * Z_BLK] = chunk(c)

    def ch_chunk(c):
        sl = slice((c - CH_CH) * Z_BLK, (c - CH_CH + 1) * Z_BLK)
        cbuf[SUBLANES:, sl] = cbuf[SUBLANES:, sl] * chunk(c)

    def gate_chunk(c):
        zg_ref[:, (c - CH_GATE) * Z_BLK:(c - CH_GATE + 1) * Z_BLK] = chunk(c).astype(BF16)

    cw = convw_ref[...]
    conv_cols = 2 * LANES

    def conv_chunk(c):
        sl = slice(c * conv_cols, (c + 1) * conv_cols)
        y = cw[2:3, sl] * cbuf[SUBLANES:, sl]
        y = y + cw[1:2, sl] * cbuf[SUBLANES - 1:SUBLANES - 1 + tm, sl]
        y = y + cw[0:1, sl] * cbuf[SUBLANES - 2:SUBLANES - 2 + tm, sl]
        cv_ref[:, sl] = (cbb[:, sl] * y).astype(BF16)

    qi = lax.broadcasted_iota(jnp.int32, (BLOCK, 2 * BLOCK), 0)
    ci = lax.broadcasted_iota(jnp.int32, (BLOCK, 2 * BLOCK), 1)
    rel = qi + BLOCK - ci
    band = (rel >= 0) & (rel < BLOCK)
    band_first = band & (ci >= jnp.where(i == 0, BLOCK, 0))

    def attn_task(b, h):
        g = h // Q_PER_KV
        rows = slice(b * BLOCK, (b + 1) * BLOCK)
        hcols = slice(h * HEAD_DIM, (h + 1) * HEAD_DIM)
        gcols = slice(g * HEAD_DIM, (g + 1) * HEAD_DIM)
        s = lax.dot_general(qbuf[rows, hcols], kbuf[b * BLOCK:(b + 2) * BLOCK, gcols],
                            (((1,), (1,)), ((), ())), preferred_element_type=F32)
        yield
        rows8 = jnp.broadcast_to(fill[h][None], (BLOCK // SUBLANES, SUBLANES, 2 * BLOCK))
        s = jnp.where(band_first if b == 0 else band, s, rows8.reshape(BLOCK, 2 * BLOCK))
        m = jnp.max(s, axis=-1, keepdims=True)
        p = jnp.exp(s - m)
        denom = jnp.sum(p, axis=-1, keepdims=True)
        w = p.astype(BF16)
        yield
        o = jnp.dot(w, vwin[b, :, gcols], preferred_element_type=F32)
        a_ref[rows, hcols] = (o * (1.0 / denom)).astype(BF16)
        yield

    tasks = [attn_task(b, h) for b in range(nblk) for h in range(N_Q_HEADS)]
    depth = 3
    n_ticks = len(tasks) + depth - 1
    ticks = iter(range(n_ticks))

    def run_ticks(count):
        for _ in range(count):
            t = next(ticks)
            for k in range(max(0, t - depth + 1), min(t + 1, len(tasks))):
                next(tasks[k])

    for c in range(CH_Q, CH_KV):
        q_chunk(c)
    kv_chunk()
    rest = ([functools.partial(cb_chunk, c) for c in range(CH_CB, CH_CC)]
            + [functools.partial(cc_chunk, c) for c in range(CH_CC, CH_CH)]
            + [functools.partial(ch_chunk, c) for c in range(CH_CH, CH_GATE)]
            + [functools.partial(gate_chunk, c) for c in range(CH_GATE, CH_END)])
    n_conv = cbuf.shape[1] // conv_cols
    span = len(rest) - 3
    for k, thunk in enumerate(rest):
        thunk()
        if k < span:
            run_ticks((k + 1) * n_ticks // span - k * n_ticks // span)
        kc = k - (CH_GATE - CH_CB)
        if 0 <= kc < n_conv:
            conv_chunk(kc)
    assert next(ticks, None) is None
    for src, dst in zip(cast_in, cast_out):
        dst[...] = src[...].astype(BF16)


def _cast_spec(w, n_steps):
    rows = w.shape[0]
    pack = 2 * SUBLANES
    blk = pack
    while blk * n_steps < rows or rows % blk or n_steps % (rows // blk):
        blk += pack
    stride = n_steps // (rows // blk)
    return pl.BlockSpec((blk, w.shape[1]), lambda i: (jnp.minimum(i, n_steps - 1) // stride, 0))


def _in_mix(x, g, w, tab, sinks, conv_w, cast_ws, tm):
    s, d = x.shape
    cast_specs = [_cast_spec(cw_, s // tm) for cw_ in cast_ws]
    n = w.shape[1]
    aw = N_Q_HEADS * HEAD_DIM
    kvw = N_KV_HEADS * HEAD_DIM
    cw = conv_w.shape[1]
    gw = n - CH_GATE * Z_BLK
    assert aw == (CH_KV - CH_Q) * Z_BLK and 2 * kvw == Z_BLK and cw == (CH_CC - CH_CB) * Z_BLK
    assert n == CH_END * Z_BLK and gw == 2 * d
    return pl.pallas_call(
        functools.partial(_in_mix_kernel, n_cast=len(cast_ws)),
        grid=(s // tm,),
        in_specs=[
            pl.BlockSpec((tm, d), lambda i: (i, 0)),
            _const_spec((1, d)),
            _const_spec((d, n)),
            pl.BlockSpec((tm, LANES), lambda i: (i, 0)),
            pl.BlockSpec(memory_space=pltpu.SMEM),
            _const_spec((CONV_K, cw)),
        ] + cast_specs,
        out_specs=[
            pl.BlockSpec((tm, aw), lambda i: (i, 0)),
            pl.BlockSpec((tm, cw), lambda i: (i, 0)),
            pl.BlockSpec((tm, gw), lambda i: (i, 0)),
        ] + cast_specs,
        out_shape=[
            jax.ShapeDtypeStruct((s, aw), BF16),
            jax.ShapeDtypeStruct((s, cw), BF16),
            jax.ShapeDtypeStruct((s, gw), BF16),
        ] + [jax.ShapeDtypeStruct(cw_.shape, BF16) for cw_ in cast_ws],
        scratch_shapes=[
            pltpu.VMEM((tm, aw), BF16),
            pltpu.VMEM((tm + BLOCK, kvw), BF16),
            pltpu.VMEM((tm + BLOCK, kvw), BF16),
            pltpu.VMEM((tm // BLOCK, 2 * BLOCK, kvw), BF16),
            pltpu.VMEM((tm + SUBLANES, cw), F32),
            pltpu.VMEM((tm, cw), F32),
            pltpu.VMEM((N_Q_HEADS, SUBLANES, 2 * BLOCK), F32),
        ],
        compiler_params=pltpu.CompilerParams(
            dimension_semantics=("arbitrary",), vmem_limit_bytes=VMEM_LIMIT),
        name="in_mix",
    )(x, g.reshape(1, d), w, tab, sinks, conv_w, *cast_ws)


def _mix_out_kernel(*refs):
    nb = N_GATE_BLKS
    a_ref, cv_ref, zg_ref, x_ref = refs[:4]
    wao_refs, wco_refs, wout_refs = (refs[4 + k * nb:4 + (k + 1) * nb] for k in range(3))
    gx_ref, wxq_ref, kvm_ref = refs[4 + 3 * nb:7 + 3 * nb]
    wxo_refs = refs[7 + 3 * nb:7 + 4 * nb]
    wgu_in, wd_in, o_ref, wgu_out, wd_out, mbuf, h1buf, obuf = refs[7 + 4 * nb:]
    d = x_ref.shape[1]
    xw = X_HEADS * X_HEAD_DIM
    i = pl.program_id(0)
    h1_cur = h1buf.at[i % 2]
    h1_prev = h1buf.at[(i + 1) % 2]

    @pl.when(i == 0)
    def _():
        h1buf[1] = jnp.zeros(h1buf.shape[1:], F32)

    def merge_chunk(c):
        sl = slice(c * Z_BLK, (c + 1) * Z_BLK)
        ya = jnp.dot(a_ref[...], wao_refs[c][...], preferred_element_type=F32)
        yc = jnp.dot(cv_ref[...], wco_refs[c][...], preferred_element_type=F32)
        ga = zg_ref[:, sl].astype(F32)
        gc = zg_ref[:, d + c * Z_BLK:d + (c + 1) * Z_BLK].astype(F32)
        mbuf[:, sl] = (_sigmoid(ga) * ya + _sigmoid(gc) * yc).astype(BF16)

    def out_chunk(c):
        sl = slice(c * Z_BLK, (c + 1) * Z_BLK)
        h1_cur[:, sl] = x_ref[:, sl] + jnp.dot(mbuf[...], wout_refs[c][...], preferred_element_type=F32)

    state = {}

    def xq():
        nrm = _rms(h1_prev[...], gx_ref[...]).astype(BF16)
        q = jnp.dot(nrm, wxq_ref[...], preferred_element_type=F32) * (X_HEAD_DIM ** -0.5)
        state["q"] = q.astype(BF16)

    def xhead(hd):
        sl = slice(hd * X_HEAD_DIM, (hd + 1) * X_HEAD_DIM)
        s = lax.dot_general(state["q"][:, sl], kvm_ref[:, sl], (((1,), (1,)), ((), ())),
                            preferred_element_type=F32)
        p = jnp.exp(s - jnp.max(s, axis=-1, keepdims=True))
        denom = jnp.sum(p, axis=-1, keepdims=True)
        o = jnp.dot(p.astype(BF16), kvm_ref[:, xw + hd * X_HEAD_DIM:xw + (hd + 1) * X_HEAD_DIM],
                    preferred_element_type=F32)
        obuf[:, sl] = (o * (1.0 / denom)).astype(BF16)

    def xout(c):
        sl = slice(c * Z_BLK, (c + 1) * Z_BLK)
        o_ref[:, sl] = h1_prev[:, sl] + jnp.dot(obuf[...], wxo_refs[c][...], preferred_element_type=F32)

    def cast_gu():
        wgu_out[...] = wgu_in[...].astype(BF16)

    def cast_d():
        wd_out[...] = wd_in[...].astype(BF16)

    part = functools.partial
    for thunk in (part(merge_chunk, 0), xq, cast_gu, part(merge_chunk, 1), part(xhead, 0),
                  part(merge_chunk, 2), part(xhead, 1), part(merge_chunk, 3), part(xhead, 2),
                  part(out_chunk, 0), part(xhead, 3), part(out_chunk, 1), part(xout, 0), cast_d,
                  part(out_chunk, 2), part(xout, 1), part(out_chunk, 3), part(xout, 2), part(xout, 3)):
        thunk()


def _mix_out(a, cv, zg, x, w_attn_o, w_conv_o, w_out, g_xattn, w_xq, kv_mem, w_xo, w_gate_up, w_down, tm):
    s, d = x.shape
    n = s // tm
    xw = w_xq.shape[1]
    assert d == N_GATE_BLKS * Z_BLK

    def cur(width):
        return pl.BlockSpec((tm, width), lambda i: (jnp.minimum(i, n - 1), 0))

    def col_block(w, k):
        return pl.BlockSpec((w.shape[0], Z_BLK), lambda i: (0, k), pipeline_mode=pl.Buffered(1))

    gu_spec = _cast_spec(w_gate_up, n)
    d_spec = _cast_spec(w_down, n)
    ws = (w_attn_o, w_conv_o, w_out)
    return pl.pallas_call(
        _mix_out_kernel,
        grid=(n + 1,),
        in_specs=[cur(a.shape[1]), cur(cv.shape[1]), cur(zg.shape[1]), cur(d)]
        + [col_block(w, k) for w in ws for k in range(N_GATE_BLKS)]
        + [_const_spec((1, d)), _const_spec((d, xw)), _const_spec(kv_mem.shape)]
        + [col_block(w_xo, k) for k in range(N_GATE_BLKS)]
        + [gu_spec, d_spec],
        out_specs=[pl.BlockSpec((tm, d), lambda i: (jnp.maximum(i - 1, 0), 0)), gu_spec, d_spec],
        out_shape=[
            jax.ShapeDtypeStruct((s, d), F32),
            jax.ShapeDtypeStruct(w_gate_up.shape, BF16),
            jax.ShapeDtypeStruct(w_down.shape, BF16),
        ],
        scratch_shapes=[
            pltpu.VMEM((tm, d), BF16),
            pltpu.VMEM((2, tm, d), F32),
            pltpu.VMEM((tm, xw), BF16),
        ],
        compiler_params=pltpu.CompilerParams(
            dimension_semantics=("arbitrary",), vmem_limit_bytes=VMEM_LIMIT),
        name="mix_out",
    )(a, cv, zg, x, *[w for w in ws for _ in range(N_GATE_BLKS)],
      g_xattn.reshape(1, d), w_xq, kv_mem, *([w_xo] * N_GATE_BLKS), w_gate_up, w_down)


def _ffn_kernel(*refs, final_norm):
    nb = N_GATE_BLKS
    h_ref, g_ref, wg_ref, wu_ref = refs[:4]
    wd_refs = refs[4:4 + nb]
    gf_ref, o_ref, n_ref = refs[4 + nb:]
    j = pl.program_id(1)

    @pl.when(j == 0)
    def _():
        h = h_ref[...]
        n_ref[...] = _rms(h, g_ref[...]).astype(BF16)
        o_ref[...] = h

    n = n_ref[...]
    g = jnp.dot(n, wg_ref[...], preferred_element_type=F32)
    u = jnp.dot(n, wu_ref[...], preferred_element_type=F32)
    act = (g * _sigmoid(g) * u).astype(BF16)
    for c in range(nb):
        sl = slice(c * Z_BLK, (c + 1) * Z_BLK)
        o_ref[:, sl] += jnp.dot(act, wd_refs[c][...], preferred_element_type=F32)

    if final_norm:
        @pl.when(j == pl.num_programs(1) - 1)
        def _():
            o_ref[...] = _rms(o_ref[...], gf_ref[...])


def _ffn(h, g, w_gate_up, w_down, g_final, tm, tf, final_norm):
    s, d = h.shape
    dff = w_down.shape[0]
    nf = dff // tf
    return pl.pallas_call(
        functools.partial(_ffn_kernel, final_norm=final_norm),
        grid=(s // tm, nf),
        in_specs=[
            pl.BlockSpec((tm, d), lambda i, j: (i, 0)),
            pl.BlockSpec((1, d), lambda i, j: (0, 0)),
            pl.BlockSpec((d, tf), lambda i, j: (0, j)),
            pl.BlockSpec((d, tf), lambda i, j: (0, j + nf)),
        ]
        + [pl.BlockSpec((tf, Z_BLK), functools.partial(lambda i, j, c: (j, c), c=c))
           for c in range(N_GATE_BLKS)]
        + [pl.BlockSpec((1, d), lambda i, j: (0, 0))],
        out_specs=pl.BlockSpec((tm, d), lambda i, j: (i, 0)),
        out_shape=jax.ShapeDtypeStruct((s, d), F32),
        scratch_shapes=[pltpu.VMEM((tm, d), BF16)],
        compiler_params=pltpu.CompilerParams(
            dimension_semantics=("arbitrary", "arbitrary"), vmem_limit_bytes=VMEM_LIMIT),
        name="ffn",
    )(h, g.reshape(1, d), w_gate_up, w_gate_up, *([w_down] * N_GATE_BLKS), g_final.reshape(1, d))


def _rope_table(s):
    inv_freq = ROPE_THETA ** (-np.arange(0, ROT_DIM, 2, dtype=np.float64) / ROT_DIM)
    ang = np.arange(s, dtype=np.float64)[:, None] * inv_freq[None, :]
    cos, sin = np.cos(ang), np.sin(ang)
    ones = np.ones((s, HEAD_DIM - ROT_DIM))
    zeros = np.zeros((s, HEAD_DIM - ROT_DIM))
    return jnp.asarray(np.concatenate([cos, cos, ones, -sin, sin, zeros], axis=1), dtype=F32)


def kernel(x, mem, g_mix, w_in, conv_w, sinks, w_attn_o, w_conv_o, w_out, g_xattn, g_mem,
           w_xq, w_xkv, w_xo, g_ffn, w_gate_up, w_down, g_final):
    batch, s, d = x.shape
    depth = w_in.shape[0]
    if depth < 1:
        raise ValueError("depth must be positive")
    tab = _rope_table(s)

    outs = []
    for b in range(batch):
        h = x[b]
        for l in range(depth):
            a, cv, zg, wao, wco, wout, wxq, wxo = _in_mix(
                h, g_mix[l], w_in[l].astype(BF16), tab, sinks[l], conv_w[l],
                (w_attn_o[l], w_conv_o[l], w_out[l], w_xq[l], w_xo[l]), tm=256)
            kv_mem = _mem_kv(mem[b], g_mem[l], w_xkv[l])
            h, wgu, wdn = _mix_out(a, cv, zg, h, wao, wco, wout, g_xattn[l], wxq, kv_mem, wxo,
                                   w_gate_up[l], w_down[l], tm=256)
            h = _ffn(h, g_ffn[l], wgu, wdn, g_final, tm=1024, tf=512, final_norm=(l == depth - 1))
        outs.append(h)
    return jnp.stack(outs, axis=0)
```

```python
import functools

import jax
import jax.numpy as jnp
import numpy as np
from jax import lax
from jax.experimental import pallas as pl
from jax.experimental.pallas import tpu as pltpu

F32 = jnp.float32
BF16 = jnp.bfloat16

EPS = 1e-6
HEAD_DIM = 64
N_Q_HEADS = 16
N_KV_HEADS = 4
Q_PER_KV = N_Q_HEADS // N_KV_HEADS
BLOCK = 128
ROT_DIM = HEAD_DIM // 4
ROPE_THETA = 500000.0
CONV_K = 3
X_HEADS = 4
X_HEAD_DIM = 128
LANES = 128
SUBLANES = 8

VMEM_LIMIT = 56 * 1024 * 1024


def _rms(x, g):
    ms = jnp.mean(x * x, axis=-1, keepdims=True)
    return x * lax.rsqrt(ms + EPS) * g


def _sigmoid(x):
    return 0.5 * jnp.tanh(0.5 * x) + 0.5


def _const_spec(shape):
    return pl.BlockSpec(shape, lambda i: (0,) * len(shape), pipeline_mode=pl.Buffered(1))


def _mem_kv_kernel(mem_ref, g_ref, w_ref, kt_ref, v_ref):
    n = _rms(mem_ref[...], g_ref[...]).astype(BF16)
    kv = jnp.dot(n, w_ref[...].astype(BF16), preferred_element_type=F32)
    xw = v_ref.shape[1]
    kt_ref[...] = kv[:, :xw].T.astype(BF16)
    v_ref[...] = kv[:, xw:].astype(BF16)


def _mem_kv(mem, g_mem, w_xkv):
    n_mem, d = mem.shape
    xw = w_xkv.shape[1] // 2
    return pl.pallas_call(
        _mem_kv_kernel,
        out_shape=[jax.ShapeDtypeStruct((xw, n_mem), BF16), jax.ShapeDtypeStruct((n_mem, xw), BF16)],
        compiler_params=pltpu.CompilerParams(vmem_limit_bytes=VMEM_LIMIT),
        name="mem_kv",
    )(mem, g_mem.reshape(1, d), w_xkv)


Z_BLK = 512
CH_Q, CH_KV, CH_CB, CH_CC, CH_CH, CH_GATE, CH_END = 0, 2, 3, 5, 7, 9, 17
N_GATE_BLKS = 4


def _in_mix_kernel(*refs, n_cast):
    x_ref, g_ref, w_ref, tab_ref, sink_ref, convw_ref = refs[:6]
    cast_in = refs[6:6 + n_cast]
    a_ref, cv_ref, zg_ref = refs[6 + n_cast:9 + n_cast]
    cast_out = refs[9 + n_cast:9 + 2 * n_cast]
    qbuf, kbuf, vbuf, vwin, cbuf, cbb, fill = refs[9 + 2 * n_cast:]
    i = pl.program_id(0)
    tm = x_ref.shape[0]
    nblk = tm // BLOCK
    kvw = N_KV_HEADS * HEAD_DIM

    @pl.when(i == 0)
    def _():
        kbuf[:, 0:BLOCK] = jnp.zeros((kvw, BLOCK), BF16)
        vbuf[0:BLOCK, :] = jnp.zeros((BLOCK, kvw), BF16)
        cbuf[0:SUBLANES, :] = jnp.zeros((SUBLANES, cbuf.shape[1]), F32)
        col = lax.broadcasted_iota(jnp.int32, (SUBLANES, 2 * BLOCK), 1)
        for h in range(N_Q_HEADS):
            fill[h] = jnp.where(col == 0, sink_ref[h], -jnp.inf)

    @pl.when(i > 0)
    def _():
        kbuf[:, 0:BLOCK] = kbuf[:, tm:tm + BLOCK]
        vbuf[0:BLOCK, :] = vbuf[tm:tm + BLOCK, :]
        cbuf[0:SUBLANES, :] = cbuf[tm:tm + SUBLANES, :]

    n = _rms(x_ref[...], g_ref[...]).astype(BF16)

    def chunk(c):
        return jnp.dot(n, w_ref[:, c * Z_BLK:(c + 1) * Z_BLK], preferred_element_type=F32)

    half = ROT_DIM // 2
    tab = tab_ref[...]
    swapped = pltpu.roll(tab, HEAD_DIM, 1)
    lane = lax.broadcasted_iota(jnp.int32, tab.shape, 1)
    low = lane < HEAD_DIM
    hl = lane & (HEAD_DIM - 1)
    cos = jnp.where(low, tab, swapped)
    ss = jnp.where(low, swapped, tab)
    sa = jnp.where(hl < half, ss, 0.0)
    sb = jnp.where(hl >= half, ss, 0.0)

    def rope(t):
        return t * cos + pltpu.roll(t, LANES - half, 1) * sa + pltpu.roll(t, half, 1) * sb

    scale = HEAD_DIM ** -0.5

    def q_chunk(c):
        r = chunk(c)
        for g in range(Z_BLK // LANES):
            col = (c - CH_Q) * Z_BLK + g * LANES
            qbuf[:, col:col + LANES] = (rope(r[:, g * LANES:(g + 1) * LANES]) * scale).astype(BF16)

    def kv_chunk():
        r = chunk(CH_KV)
        for g in range(kvw // LANES):
            sl = slice(g * LANES, (g + 1) * LANES)
            kbuf[sl, BLOCK:] = rope(r[:, sl]).T.astype(BF16)
        vbuf[BLOCK:, :] = r[:, kvw:2 * kvw].astype(BF16)
        pack = 2 * SUBLANES
        row = lax.broadcasted_iota(jnp.int32, (pack, kvw), 0)
        for b in range(nblk):
            r0 = b * BLOCK
            head = vbuf[r0:r0 + pack, :].astype(F32)
            vwin[b, 0:pack, :] = jnp.where(row == 0, 0.0, head).astype(BF16)
            vwin[b, pack:, :] = vbuf[r0 + pack:r0 + 2 * BLOCK, :]

    def cb_chunk(c):
        cbb[:, (c - CH_CB) * Z_BLK:(c - CH_CB + 1) * Z_BLK] = chunk(c)

    def cc_chunk(c):
        cbuf[SUBLANES:, (c - CH_CC) * Z_BLK:(c - CH_CC + 1) * Z_BLK] = chunk(c)

    def ch_chunk(c):
        sl = slice((c - CH_CH) * Z_BLK, (c - CH_CH + 1) * Z_BLK)
        cbuf[SUBLANES:, sl] = cbuf[SUBLANES:, sl] * chunk(c)

    def gate_chunk(c):
        zg_ref[:, (c - CH_GATE) * Z_BLK:(c - CH_GATE + 1) * Z_BLK] = _sigmoid(chunk(c)).astype(BF16)

    cw = convw_ref[...]
    conv_cols = 2 * LANES

    def conv_chunk(c):
        sl = slice(c * conv_cols, (c + 1) * conv_cols)
        y = cw[2:3, sl] * cbuf[SUBLANES:, sl]
        y = y + cw[1:2, sl] * cbuf[SUBLANES - 1:SUBLANES - 1 + tm, sl]
        y = y + cw[0:1, sl] * cbuf[SUBLANES - 2:SUBLANES - 2 + tm, sl]
        cv_ref[:, sl] = (cbb[:, sl] * y).astype(BF16)

    qi = lax.broadcasted_iota(jnp.int32, (BLOCK, 2 * BLOCK), 0)
    ci = lax.broadcasted_iota(jnp.int32, (BLOCK, 2 * BLOCK), 1)
    rel = qi + BLOCK - ci
    band = (rel >= 0) & (rel < BLOCK)
    band_first = band & (ci >= jnp.where(i == 0, BLOCK, 0))

    def attn_task(b, h):
        g = h // Q_PER_KV
        rows = slice(b * BLOCK, (b + 1) * BLOCK)
        hcols = slice(h * HEAD_DIM, (h + 1) * HEAD_DIM)
        gcols = slice(g * HEAD_DIM, (g + 1) * HEAD_DIM)
        s = jnp.dot(qbuf[rows, hcols], kbuf[gcols, b * BLOCK:(b + 2) * BLOCK], preferred_element_type=F32)
        yield
        rows8 = jnp.broadcast_to(fill[h][None], (BLOCK // SUBLANES, SUBLANES, 2 * BLOCK))
        s = jnp.where(band_first if b == 0 else band, s, rows8.reshape(BLOCK, 2 * BLOCK))
        m = jnp.max(s, axis=-1, keepdims=True)
        p = jnp.exp(s - m)
        denom = jnp.sum(p, axis=-1, keepdims=True)
        w = p.astype(BF16)
        yield
        o = jnp.dot(w, vwin[b, :, gcols], preferred_element_type=F32)
        a_ref[rows, hcols] = (o * (1.0 / denom)).astype(BF16)
        yield

    tasks = [attn_task(b, h) for b in range(nblk) for h in range(N_Q_HEADS)]
    depth = 3
    n_ticks = len(tasks) + depth - 1
    ticks = iter(range(n_ticks))

    def run_ticks(count):
        for _ in range(count):
            t = next(ticks)
            for k in range(max(0, t - depth + 1), min(t + 1, len(tasks))):
                next(tasks[k])

    for c in range(CH_Q, CH_KV):
        q_chunk(c)
    kv_chunk()
    rest = ([functools.partial(cb_chunk, c) for c in range(CH_CB, CH_CC)]
            + [functools.partial(cc_chunk, c) for c in range(CH_CC, CH_CH)]
            + [functools.partial(ch_chunk, c) for c in range(CH_CH, CH_GATE)]
            + [functools.partial(gate_chunk, c) for c in range(CH_GATE, CH_END)])
    n_conv = cbuf.shape[1] // conv_cols
    span = len(rest) - 3
    for k, thunk in enumerate(rest):
        thunk()
        if k < span:
            run_ticks((k + 1) * n_ticks // span - k * n_ticks // span)
        kc = k - (CH_GATE - CH_CB)
        if 0 <= kc < n_conv:
            conv_chunk(kc)
    assert next(ticks, None) is None
    for src, dst in zip(cast_in, cast_out):
        dst[...] = src[...].astype(BF16)


def _cast_spec(w, n_steps):
    rows = w.shape[0]
    pack = 2 * SUBLANES
    blk = pack
    while blk * n_steps < rows or rows % blk or n_steps % (rows // blk):
        blk += pack
    stride = n_steps // (rows // blk)
    return pl.BlockSpec((blk, w.shape[1]), lambda i: (jnp.minimum(i, n_steps - 1) // stride, 0))


def _in_mix(x, g, w, tab, sinks, conv_w, cast_ws, tm):
    s, d = x.shape
    cast_specs = [_cast_spec(cw_, s // tm) for cw_ in cast_ws]
    n = w.shape[1]
    aw = N_Q_HEADS * HEAD_DIM
    kvw = N_KV_HEADS * HEAD_DIM
    cw = conv_w.shape[1]
    gw = n - CH_GATE * Z_BLK
    assert aw == (CH_KV - CH_Q) * Z_BLK and 2 * kvw == Z_BLK and cw == (CH_CC - CH_CB) * Z_BLK
    assert n == CH_END * Z_BLK and gw == 2 * d
    return pl.pallas_call(
        functools.partial(_in_mix_kernel, n_cast=len(cast_ws)),
        grid=(s // tm,),
        in_specs=[
            pl.BlockSpec((tm, d), lambda i: (i, 0)),
            _const_spec((1, d)),
            _const_spec((d, n)),
            pl.BlockSpec((tm, LANES), lambda i: (i, 0)),
            pl.BlockSpec(memory_space=pltpu.SMEM),
            _const_spec((CONV_K, cw)),
        ] + cast_specs,
        out_specs=[
            pl.BlockSpec((tm, aw), lambda i: (i, 0)),
            pl.BlockSpec((tm, cw), lambda i: (i, 0)),
            pl.BlockSpec((tm, gw), lambda i: (i, 0)),
        ] + cast_specs,
        out_shape=[
            jax.ShapeDtypeStruct((s, aw), BF16),
            jax.ShapeDtypeStruct((s, cw), BF16),
            jax.ShapeDtypeStruct((s, gw), BF16),
        ] + [jax.ShapeDtypeStruct(cw_.shape, BF16) for cw_ in cast_ws],
        scratch_shapes=[
            pltpu.VMEM((tm, aw), BF16),
            pltpu.VMEM((kvw, tm + BLOCK), BF16),
            pltpu.VMEM((tm + BLOCK, kvw), BF16),
            pltpu.VMEM((tm // BLOCK, 2 * BLOCK, kvw), BF16),
            pltpu.VMEM((tm + SUBLANES, cw), F32),
            pltpu.VMEM((tm, cw), F32),
            pltpu.VMEM((N_Q_HEADS, SUBLANES, 2 * BLOCK), F32),
        ],
        compiler_params=pltpu.CompilerParams(
            dimension_semantics=("arbitrary",), vmem_limit_bytes=VMEM_LIMIT),
        name="in_mix",
    )(x, g.reshape(1, d), w, tab, sinks, conv_w, *cast_ws)


def _mix_out_kernel(*refs):
    nb = N_GATE_BLKS
    a_ref, cv_ref, zg_ref, x_ref = refs[:4]
    wao_refs, wco_refs, wout_refs = (refs[4 + k * nb:4 + (k + 1) * nb] for k in range(3))
    gx_ref, wxq_ref, kt_ref, vm_ref = refs[4 + 3 * nb:8 + 3 * nb]
    wxo_refs = refs[8 + 3 * nb:8 + 4 * nb]
    wgu_in, wd_in, o_ref, wgu_out, wd_out, mbuf, h1buf, obuf = refs[8 + 4 * nb:]
    d = x_ref.shape[1]
    i = pl.program_id(0)
    h1_cur = h1buf.at[i % 2]
    h1_prev = h1buf.at[(i + 1) % 2]

    @pl.when(i == 0)
    def _():
        h1buf[1] = jnp.zeros(h1buf.shape[1:], F32)

    def merge_chunk(c):
        sl = slice(c * Z_BLK, (c + 1) * Z_BLK)
        ya = jnp.dot(a_ref[...], wao_refs[c][...], preferred_element_type=F32)
        yc = jnp.dot(cv_ref[...], wco_refs[c][...], preferred_element_type=F32)
        ga = zg_ref[:, sl].astype(F32)
        gc = zg_ref[:, d + c * Z_BLK:d + (c + 1) * Z_BLK].astype(F32)
        mbuf[:, sl] = (ga * ya + gc * yc).astype(BF16)

    def out_chunk(c):
        sl = slice(c * Z_BLK, (c + 1) * Z_BLK)
        h1_cur[:, sl] = x_ref[:, sl] + jnp.dot(mbuf[...], wout_refs[c][...], preferred_element_type=F32)

    state = {}

    def xq():
        nrm = _rms(h1_prev[...], gx_ref[...]).astype(BF16)
        q = jnp.dot(nrm, wxq_ref[...], preferred_element_type=F32) * (X_HEAD_DIM ** -0.5)
        state["q"] = q.astype(BF16)

    def xhead(hd):
        sl = slice(hd * X_HEAD_DIM, (hd + 1) * X_HEAD_DIM)
        s = jnp.dot(state["q"][:, sl], kt_ref[sl, :], preferred_element_type=F32)
        p = jnp.exp(s - jnp.max(s, axis=-1, keepdims=True))
        denom = jnp.sum(p, axis=-1, keepdims=True)
        o = jnp.dot(p.astype(BF16), vm_ref[:, sl], preferred_element_type=F32)
        obuf[:, sl] = (o * (1.0 / denom)).astype(BF16)

    def xout(c):
        sl = slice(c * Z_BLK, (c + 1) * Z_BLK)
        o_ref[:, sl] = h1_prev[:, sl] + jnp.dot(obuf[...], wxo_refs[c][...], preferred_element_type=F32)

    def cast_gu():
        wgu_out[...] = wgu_in[...].astype(BF16)

    def cast_d():
        wd_out[...] = wd_in[...].astype(BF16)

    part = functools.partial
    for thunk in (part(merge_chunk, 0), xq, cast_gu, part(merge_chunk, 1), part(xhead, 0),
                  part(merge_chunk, 2), part(xhead, 1), part(merge_chunk, 3), part(xhead, 2),
                  part(out_chunk, 0), part(xhead, 3), part(out_chunk, 1), part(xout, 0), cast_d,
                  part(out_chunk, 2), part(xout, 1), part(out_chunk, 3), part(xout, 2), part(xout, 3)):
        thunk()


def _mix_out(a, cv, zg, x, w_attn_o, w_conv_o, w_out, g_xattn, w_xq, kt_mem, v_mem, w_xo, w_gate_up, w_down,
             tm):
    s, d = x.shape
    n = s // tm
    xw = w_xq.shape[1]
    assert d == N_GATE_BLKS * Z_BLK

    def cur(width):
        return pl.BlockSpec((tm, width), lambda i: (jnp.minimum(i, n - 1), 0))

    def col_block(w, k):
        return pl.BlockSpec((w.shape[0], Z_BLK), lambda i: (0, k), pipeline_mode=pl.Buffered(1))

    gu_spec = _cast_spec(w_gate_up, n)
    d_spec = _cast_spec(w_down, n)
    ws = (w_attn_o, w_conv_o, w_out)
    return pl.pallas_call(
        _mix_out_kernel,
        grid=(n + 1,),
        in_specs=[cur(a.shape[1]), cur(cv.shape[1]), cur(zg.shape[1]), cur(d)]
        + [col_block(w, k) for w in ws for k in range(N_GATE_BLKS)]
        + [_const_spec((1, d)), _const_spec((d, xw)), _const_spec(kt_mem.shape), _const_spec(v_mem.shape)]
        + [col_block(w_xo, k) for k in range(N_GATE_BLKS)]
        + [gu_spec, d_spec],
        out_specs=[pl.BlockSpec((tm, d), lambda i: (jnp.maximum(i - 1, 0), 0)), gu_spec, d_spec],
        out_shape=[
            jax.ShapeDtypeStruct((s, d), F32),
            jax.ShapeDtypeStruct(w_gate_up.shape, BF16),
            jax.ShapeDtypeStruct(w_down.shape, BF16),
        ],
        scratch_shapes=[
            pltpu.VMEM((tm, d), BF16),
            pltpu.VMEM((2, tm, d), F32),
            pltpu.VMEM((tm, xw), BF16),
        ],
        compiler_params=pltpu.CompilerParams(
            dimension_semantics=("arbitrary",), vmem_limit_bytes=VMEM_LIMIT),
        name="mix_out",
    )(a, cv, zg, x, *[w for w in ws for _ in range(N_GATE_BLKS)],
      g_xattn.reshape(1, d), w_xq, kt_mem, v_mem, *([w_xo] * N_GATE_BLKS), w_gate_up, w_down)


def _ffn_kernel(*refs, final_norm):
    nb = N_GATE_BLKS
    h_ref, g_ref, wg_ref, wu_ref = refs[:4]
    wd_refs = refs[4:4 + nb]
    gf_ref, o_ref, n_ref = refs[4 + nb:]
    j = pl.program_id(1)

    @pl.when(j == 0)
    def _():
        h = h_ref[...]
        n_ref[...] = _rms(h, g_ref[...]).astype(BF16)
        o_ref[...] = h

    n = n_ref[...]
    g = jnp.dot(n, wg_ref[...], preferred_element_type=F32)
    u = jnp.dot(n, wu_ref[...], preferred_element_type=F32)
    act = (g * _sigmoid(g) * u).astype(BF16)
    for c in range(nb):
        sl = slice(c * Z_BLK, (c + 1) * Z_BLK)
        o_ref[:, sl] += jnp.dot(act, wd_refs[c][...], preferred_element_type=F32)

    if final_norm:
        @pl.when(j == pl.num_programs(1) - 1)
        def _():
            o_ref[...] = _rms(o_ref[...], gf_ref[...])


def _ffn(h, g, w_gate_up, w_down, g_final, tm, tf, final_norm):
    s, d = h.shape
    dff = w_down.shape[0]
    nf = dff // tf
    return pl.pallas_call(
        functools.partial(_ffn_kernel, final_norm=final_norm),
        grid=(s // tm, nf),
        in_specs=[
            pl.BlockSpec((tm, d), lambda i, j: (i, 0)),
            pl.BlockSpec((1, d), lambda i, j: (0, 0)),
            pl.BlockSpec((d, tf), lambda i, j: (0, j)),
            pl.BlockSpec((d, tf), lambda i, j: (0, j + nf)),
        ]
        + [pl.BlockSpec((tf, Z_BLK), functools.partial(lambda i, j, c: (j, c), c=c))
           for c in range(N_GATE_BLKS)]
        + [pl.BlockSpec((1, d), lambda i, j: (0, 0))],
        out_specs=pl.BlockSpec((tm, d), lambda i, j: (i, 0)),
        out_shape=jax.ShapeDtypeStruct((s, d), F32),
        scratch_shapes=[pltpu.VMEM((tm, d), BF16)],
        compiler_params=pltpu.CompilerParams(
            dimension_semantics=("arbitrary", "arbitrary"), vmem_limit_bytes=VMEM_LIMIT),
        name="ffn",
    )(h, g.reshape(1, d), w_gate_up, w_gate_up, *([w_down] * N_GATE_BLKS), g_final.reshape(1, d))


def _rope_table(s):
    inv_freq = ROPE_THETA ** (-np.arange(0, ROT_DIM, 2, dtype=np.float64) / ROT_DIM)
    ang = np.arange(s, dtype=np.float64)[:, None] * inv_freq[None, :]
    cos, sin = np.cos(ang), np.sin(ang)
    ones = np.ones((s, HEAD_DIM - ROT_DIM))
    zeros = np.zeros((s, HEAD_DIM - ROT_DIM))
    return jnp.asarray(np.concatenate([cos, cos, ones, -sin, sin, zeros], axis=1), dtype=F32)


def kernel(x, mem, g_mix, w_in, conv_w, sinks, w_attn_o, w_conv_o, w_out, g_xattn, g_mem,
           w_xq, w_xkv, w_xo, g_ffn, w_gate_up, w_down, g_final):
    batch, s, d = x.shape
    depth = w_in.shape[0]
    if depth < 1:
        raise ValueError("depth must be positive")
    tab = _rope_table(s)

    outs = []
    for b in range(batch):
        h = x[b]
        for l in range(depth):
            a, cv, zg, wao, wco, wout, wxq, wxo = _in_mix(
                h, g_mix[l], w_in[l].astype(BF16), tab, sinks[l], conv_w[l],
                (w_attn_o[l], w_conv_o[l], w_out[l], w_xq[l], w_xo[l]), tm=256)
            kt_mem, v_mem = _mem_kv(mem[b], g_mem[l], w_xkv[l])
            h, wgu, wdn = _mix_out(a, cv, zg, h, wao, wco, wout, g_xattn[l], wxq, kt_mem, v_mem, wxo,
                                   w_gate_up[l], w_down[l], tm=256)
            h = _ffn(h, g_ffn[l], wgu, wdn, g_final, tm=1024, tf=512, final_norm=(l == depth - 1))
        outs.append(h)
    return jnp.stack(outs, axis=0)
```

```python
import functools

import jax
import jax.numpy as jnp
import numpy as np
from jax import lax
from jax.experimental import pallas as pl
from jax.experimental.pallas import tpu as pltpu

F32 = jnp.float32
BF16 = jnp.bfloat16

EPS = 1e-6
HEAD_DIM = 64
N_Q_HEADS = 16
N_KV_HEADS = 4
Q_PER_KV = N_Q_HEADS // N_KV_HEADS
BLOCK = 128
ROT_DIM = HEAD_DIM // 4
ROPE_THETA = 500000.0
CONV_K = 3
X_HEADS = 4
X_HEAD_DIM = 128
LANES = 128
SUBLANES = 8

VMEM_LIMIT = 56 * 1024 * 1024


def _rms(x, g):
    ms = jnp.mean(x * x, axis=-1, keepdims=True)
    return x * lax.rsqrt(ms + EPS) * g


def _sigmoid(x):
    return 0.5 * jnp.tanh(0.5 * x) + 0.5


def _const_spec(shape):
    return pl.BlockSpec(shape, lambda i: (0,) * len(shape), pipeline_mode=pl.Buffered(1))


def _mem_kv_kernel(mem_ref, g_ref, w_ref, kt_ref, v_ref):
    n = _rms(mem_ref[...], g_ref[...]).astype(BF16)
    kv = jnp.dot(n, w_ref[...].astype(BF16), preferred_element_type=F32)
    xw = v_ref.shape[1]
    kt_ref[...] = kv[:, :xw].T.astype(BF16)
    v_ref[...] = kv[:, xw:].astype(BF16)


def _mem_kv(mem, g_mem, w_xkv):
    n_mem, d = mem.shape
    xw = w_xkv.shape[1] // 2
    return pl.pallas_call(
        _mem_kv_kernel,
        out_shape=[jax.ShapeDtypeStruct((xw, n_mem), BF16), jax.ShapeDtypeStruct((n_mem, xw), BF16)],
        compiler_params=pltpu.CompilerParams(vmem_limit_bytes=VMEM_LIMIT),
        name="mem_kv",
    )(mem, g_mem.reshape(1, d), w_xkv)


Z_BLK = 512
CH_Q, CH_KV, CH_CB, CH_CC, CH_CH, CH_GATE, CH_END = 0, 2, 3, 5, 7, 9, 17
N_GATE_BLKS = 4


def _in_mix_kernel(*refs, n_cast):
    x_ref, g_ref, w_ref, tab_ref, sink_ref, convw_ref = refs[:6]
    cast_in = refs[6:6 + n_cast]
    a_ref, cv_ref, zg_ref = refs[6 + n_cast:9 + n_cast]
    cast_out = refs[9 + n_cast:9 + 2 * n_cast]
    qbuf, kbuf, vbuf, vwin, cbuf, cbb = refs[9 + 2 * n_cast:]
    i = pl.program_id(0)
    tm = x_ref.shape[0]
    nblk = tm // BLOCK
    kvw = N_KV_HEADS * HEAD_DIM

    @pl.when(i == 0)
    def _():
        kbuf[0:BLOCK, :] = jnp.zeros((BLOCK, kvw), BF16)
        vbuf[:, 0:BLOCK] = jnp.zeros((kvw, BLOCK), BF16)
        cbuf[0:SUBLANES, :] = jnp.zeros((SUBLANES, cbuf.shape[1]), F32)

    @pl.when(i > 0)
    def _():
        kbuf[0:BLOCK, :] = kbuf[tm:tm + BLOCK, :]
        vbuf[:, 0:BLOCK] = vbuf[:, tm:tm + BLOCK]
        cbuf[0:SUBLANES, :] = cbuf[tm:tm + SUBLANES, :]

    n = _rms(x_ref[...], g_ref[...]).astype(BF16)

    def chunk(c):
        return jnp.dot(n, w_ref[:, c * Z_BLK:(c + 1) * Z_BLK], preferred_element_type=F32)

    half = ROT_DIM // 2
    tab = tab_ref[...]
    swapped = pltpu.roll(tab, HEAD_DIM, 1)
    lane = lax.broadcasted_iota(jnp.int32, tab.shape, 1)
    low = lane < HEAD_DIM
    hl = lane & (HEAD_DIM - 1)
    cos = jnp.where(low, tab, swapped)
    ss = jnp.where(low, swapped, tab)
    sa = jnp.where(hl < half, ss, 0.0)
    sb = jnp.where(hl >= half, ss, 0.0)

    def rope(t):
        return t * cos + pltpu.roll(t, LANES - half, 1) * sa + pltpu.roll(t, half, 1) * sb

    scale = HEAD_DIM ** -0.5

    def q_chunk(c):
        r = chunk(c)
        for g in range(Z_BLK // LANES):
            col = (c - CH_Q) * Z_BLK + g * LANES
            qbuf[col:col + LANES, :] = (rope(r[:, g * LANES:(g + 1) * LANES]) * scale).T.astype(BF16)

    def kv_chunk():
        r = chunk(CH_KV)
        for g in range(kvw // LANES):
            sl = slice(g * LANES, (g + 1) * LANES)
            kbuf[BLOCK:, sl] = rope(r[:, sl]).astype(BF16)
            vbuf[sl, BLOCK:] = r[:, kvw + g * LANES:kvw + (g + 1) * LANES].T.astype(BF16)
        lane = lax.broadcasted_iota(jnp.int32, (kvw, BLOCK), 1)
        for b in range(nblk):
            first = vbuf[:, b * BLOCK:(b + 1) * BLOCK].astype(F32)
            vwin[b, :, 0:BLOCK] = jnp.where(lane == 0, 0.0, first).astype(BF16)
            vwin[b, :, BLOCK:] = vbuf[:, (b + 1) * BLOCK:(b + 2) * BLOCK]

    def cb_chunk(c):
        cbb[:, (c - CH_CB) * Z_BLK:(c - CH_CB + 1) * Z_BLK] = chunk(c)

    def cc_chunk(c):
        cbuf[SUBLANES:, (c - CH_CC) * Z_BLK:(c - CH_CC + 1) * Z_BLK] = chunk(c)

    def ch_chunk(c):
        sl = slice((c - CH_CH) * Z_BLK, (c - CH_CH + 1) * Z_BLK)
        cbuf[SUBLANES:, sl] = cbuf[SUBLANES:, sl] * chunk(c)

    def gate_chunk(c):
        zg_ref[:, (c - CH_GATE) * Z_BLK:(c - CH_GATE + 1) * Z_BLK] = _sigmoid(chunk(c)).astype(BF16)

    cw = convw_ref[...]
    conv_cols = 2 * LANES

    def conv_chunk(c):
        sl = slice(c * conv_cols, (c + 1) * conv_cols)
        y = cw[2:3, sl] * cbuf[SUBLANES:, sl]
        y = y + cw[1:2, sl] * cbuf[SUBLANES - 1:SUBLANES - 1 + tm, sl]
        y = y + cw[0:1, sl] * cbuf[SUBLANES - 2:SUBLANES - 2 + tm, sl]
        cv_ref[:, sl] = (cbb[:, sl] * y).astype(BF16)

    pair_w = 2 * BLOCK
    ci = lax.broadcasted_iota(jnp.int32, (2 * BLOCK, pair_w), 0)
    qi = lax.broadcasted_iota(jnp.int32, (2 * BLOCK, pair_w), 1) & (BLOCK - 1)
    rel = qi + BLOCK - ci
    band = (rel >= 0) & (rel < BLOCK)
    band_first = band & (ci >= jnp.where(i == 0, BLOCK, 0))
    top_row = lax.broadcasted_iota(jnp.int32, (SUBLANES, pair_w), 0) == 0
    top_second = lax.broadcasted_iota(jnp.int32, (SUBLANES, pair_w), 1) >= BLOCK

    def attn_task(b, h0):
        g = h0 // Q_PER_KV
        rows = slice(b * BLOCK, (b + 1) * BLOCK)
        gcols = slice(g * HEAD_DIM, (g + 1) * HEAD_DIM)
        qcat = jnp.concatenate([qbuf[h * HEAD_DIM:(h + 1) * HEAD_DIM, rows] for h in (h0, h0 + 1)], axis=1)
        s = jnp.dot(kbuf[b * BLOCK:(b + 2) * BLOCK, gcols], qcat, preferred_element_type=F32)
        yield
        s = jnp.where(band_first if b == 0 else band, s, -jnp.inf)
        sink = jnp.where(top_second, sink_ref[h0 + 1], sink_ref[h0])
        s = jnp.concatenate([jnp.where(top_row, sink, s[0:SUBLANES]), s[SUBLANES:]], axis=0)
        m = jnp.max(s, axis=0, keepdims=True)
        p = jnp.exp(s - m)
        denom = jnp.sum(p, axis=0, keepdims=True)
        w = p.astype(BF16)
        yield
        o = jnp.dot(vwin[b, gcols, :], w, preferred_element_type=F32) * (1.0 / denom)
        both = jnp.concatenate([o[:, :BLOCK], o[:, BLOCK:]], axis=0)
        a_ref[rows, h0 * HEAD_DIM:(h0 + 2) * HEAD_DIM] = both.T.astype(BF16)
        yield

    tasks = [attn_task(b, h0) for b in range(nblk) for h0 in range(0, N_Q_HEADS, 2)]
    depth = 3
    n_ticks = len(tasks) + depth - 1
    ticks = iter(range(n_ticks))

    def run_ticks(count):
        for _ in range(count):
            t = next(ticks)
            for k in range(max(0, t - depth + 1), min(t + 1, len(tasks))):
                next(tasks[k])

    for c in range(CH_Q, CH_KV):
        q_chunk(c)
    kv_chunk()
    rest = ([functools.partial(cb_chunk, c) for c in range(CH_CB, CH_CC)]
            + [functools.partial(cc_chunk, c) for c in range(CH_CC, CH_CH)]
            + [functools.partial(ch_chunk, c) for c in range(CH_CH, CH_GATE)]
            + [functools.partial(gate_chunk, c) for c in range(CH_GATE, CH_END)])
    n_conv = cbuf.shape[1] // conv_cols
    span = len(rest) - 3
    for k, thunk in enumerate(rest):
        thunk()
        if k < span:
            run_ticks((k + 1) * n_ticks // span - k * n_ticks // span)
        kc = k - (CH_GATE - CH_CB)
        if 0 <= kc < n_conv:
            conv_chunk(kc)
    assert next(ticks, None) is None
    for src, dst in zip(cast_in, cast_out):
        dst[...] = src[...].astype(BF16)


def _cast_spec(w, n_steps):
    rows = w.shape[0]
    pack = 2 * SUBLANES
    blk = pack
    while blk * n_steps < rows or rows % blk or n_steps % (rows // blk):
        blk += pack
    stride = n_steps // (rows // blk)
    return pl.BlockSpec((blk, w.shape[1]), lambda i: (jnp.minimum(i, n_steps - 1) // stride, 0))


def _in_mix(x, g, w, tab, sinks, conv_w, cast_ws, tm):
    s, d = x.shape
    cast_specs = [_cast_spec(cw_, s // tm) for cw_ in cast_ws]
    n = w.shape[1]
    aw = N_Q_HEADS * HEAD_DIM
    kvw = N_KV_HEADS * HEAD_DIM
    cw = conv_w.shape[1]
    gw = n - CH_GATE * Z_BLK
    assert aw == (CH_KV - CH_Q) * Z_BLK and 2 * kvw == Z_BLK and cw == (CH_CC - CH_CB) * Z_BLK
    assert n == CH_END * Z_BLK and gw == 2 * d
    return pl.pallas_call(
        functools.partial(_in_mix_kernel, n_cast=len(cast_ws)),
        grid=(s // tm,),
        in_specs=[
            pl.BlockSpec((tm, d), lambda i: (i, 0)),
            _const_spec((1, d)),
            _const_spec((d, n)),
            pl.BlockSpec((tm, LANES), lambda i: (i, 0)),
            pl.BlockSpec(memory_space=pltpu.SMEM),
            _const_spec((CONV_K, cw)),
        ] + cast_specs,
        out_specs=[
            pl.BlockSpec((tm, aw), lambda i: (i, 0)),
            pl.BlockSpec((tm, cw), lambda i: (i, 0)),
            pl.BlockSpec((tm, gw), lambda i: (i, 0)),
        ] + cast_specs,
        out_shape=[
            jax.ShapeDtypeStruct((s, aw), BF16),
            jax.ShapeDtypeStruct((s, cw), BF16),
            jax.ShapeDtypeStruct((s, gw), BF16),
        ] + [jax.ShapeDtypeStruct(cw_.shape, BF16) for cw_ in cast_ws],
        scratch_shapes=[
            pltpu.VMEM((aw, tm), BF16),
            pltpu.VMEM((tm + BLOCK, kvw), BF16),
            pltpu.VMEM((kvw, tm + BLOCK), BF16),
            pltpu.VMEM((tm // BLOCK, kvw, 2 * BLOCK), BF16),
            pltpu.VMEM((tm + SUBLANES, cw), F32),
            pltpu.VMEM((tm, cw), F32),
        ],
        compiler_params=pltpu.CompilerParams(
            dimension_semantics=("arbitrary",), vmem_limit_bytes=VMEM_LIMIT),
        name="in_mix",
    )(x, g.reshape(1, d), w, tab, sinks, conv_w, *cast_ws)


def _mix_out_kernel(*refs):
    nb = N_GATE_BLKS
    a_ref, cv_ref, zg_ref, x_ref = refs[:4]
    wao_refs, wco_refs, wout_refs = (refs[4 + k * nb:4 + (k + 1) * nb] for k in range(3))
    gx_ref, wxq_ref, kt_ref, vm_ref = refs[4 + 3 * nb:8 + 3 * nb]
    wxo_refs = refs[8 + 3 * nb:8 + 4 * nb]
    wgu_in, wd_in, o_ref, wgu_out, wd_out, mbuf, h1buf, obuf = refs[8 + 4 * nb:]
    d = x_ref.shape[1]
    i = pl.program_id(0)
    h1_cur = h1buf.at[i % 2]
    h1_prev = h1buf.at[(i + 1) % 2]

    @pl.when(i == 0)
    def _():
        h1buf[1] = jnp.zeros(h1buf.shape[1:], F32)

    def merge_chunk(c):
        sl = slice(c * Z_BLK, (c + 1) * Z_BLK)
        ya = jnp.dot(a_ref[...], wao_refs[c][...], preferred_element_type=F32)
        yc = jnp.dot(cv_ref[...], wco_refs[c][...], preferred_element_type=F32)
        ga = zg_ref[:, sl].astype(F32)
        gc = zg_ref[:, d + c * Z_BLK:d + (c + 1) * Z_BLK].astype(F32)
        mbuf[:, sl] = (ga * ya + gc * yc).astype(BF16)

    def out_chunk(c):
        sl = slice(c * Z_BLK, (c + 1) * Z_BLK)
        h1_cur[:, sl] = x_ref[:, sl] + jnp.dot(mbuf[...], wout_refs[c][...], preferred_element_type=F32)

    state = {}

    def xq():
        nrm = _rms(h1_prev[...], gx_ref[...]).astype(BF16)
        q = jnp.dot(nrm, wxq_ref[...], preferred_element_type=F32) * (X_HEAD_DIM ** -0.5)
        state["q"] = q.astype(BF16)

    def xhead(hd):
        sl = slice(hd * X_HEAD_DIM, (hd + 1) * X_HEAD_DIM)
        s = jnp.dot(state["q"][:, sl], kt_ref[sl, :], preferred_element_type=F32)
        p = jnp.exp(s - jnp.max(s, axis=-1, keepdims=True))
        denom = jnp.sum(p, axis=-1, keepdims=True)
        o = jnp.dot(p.astype(BF16), vm_ref[:, sl], preferred_element_type=F32)
        obuf[:, sl] = (o * (1.0 / denom)).astype(BF16)

    def xout(c):
        sl = slice(c * Z_BLK, (c + 1) * Z_BLK)
        o_ref[:, sl] = h1_prev[:, sl] + jnp.dot(obuf[...], wxo_refs[c][...], preferred_element_type=F32)

    def cast_gu():
        wgu_out[...] = wgu_in[...].astype(BF16)

    def cast_d():
        wd_out[...] = wd_in[...].astype(BF16)

    part = functools.partial
    for thunk in (part(merge_chunk, 0), xq, cast_gu, part(merge_chunk, 1), part(xhead, 0),
                  part(merge_chunk, 2), part(xhead, 1), part(merge_chunk, 3), part(xhead, 2),
                  part(out_chunk, 0), part(xhead, 3), part(out_chunk, 1), part(xout, 0), cast_d,
                  part(out_chunk, 2), part(xout, 1), part(out_chunk, 3), part(xout, 2), part(xout, 3)):
        thunk()


def _mix_out(a, cv, zg, x, w_attn_o, w_conv_o, w_out, g_xattn, w_xq, kt_mem, v_mem, w_xo, w_gate_up, w_down,
             tm):
    s, d = x.shape
    n = s // tm
    xw = w_xq.shape[1]
    assert d == N_GATE_BLKS * Z_BLK

    def cur(width):
        return pl.BlockSpec((tm, width), lambda i: (jnp.minimum(i, n - 1), 0))

    def col_block(w, k):
        return pl.BlockSpec((w.shape[0], Z_BLK), lambda i: (0, k), pipeline_mode=pl.Buffered(1))

    gu_spec = _cast_spec(w_gate_up, n)
    d_spec = _cast_spec(w_down, n)
    ws = (w_attn_o, w_conv_o, w_out)
    return pl.pallas_call(
        _mix_out_kernel,
        grid=(n + 1,),
        in_specs=[cur(a.shape[1]), cur(cv.shape[1]), cur(zg.shape[1]), cur(d)]
        + [col_block(w, k) for w in ws for k in range(N_GATE_BLKS)]
        + [_const_spec((1, d)), _const_spec((d, xw)), _const_spec(kt_mem.shape), _const_spec(v_mem.shape)]
        + [col_block(w_xo, k) for k in range(N_GATE_BLKS)]
        + [gu_spec, d_spec],
        out_specs=[pl.BlockSpec((tm, d), lambda i: (jnp.maximum(i - 1, 0), 0)), gu_spec, d_spec],
        out_shape=[
            jax.ShapeDtypeStruct((s, d), F32),
            jax.ShapeDtypeStruct(w_gate_up.shape, BF16),
            jax.ShapeDtypeStruct(w_down.shape, BF16),
        ],
        scratch_shapes=[
            pltpu.VMEM((tm, d), BF16),
            pltpu.VMEM((2, tm, d), F32),
            pltpu.VMEM((tm, xw), BF16),
        ],
        compiler_params=pltpu.CompilerParams(
            dimension_semantics=("arbitrary",), vmem_limit_bytes=VMEM_LIMIT),
        name="mix_out",
    )(a, cv, zg, x, *[w for w in ws for _ in range(N_GATE_BLKS)],
      g_xattn.reshape(1, d), w_xq, kt_mem, v_mem, *([w_xo] * N_GATE_BLKS), w_gate_up, w_down)


def _ffn_kernel(*refs, final_norm):
    nb = N_GATE_BLKS
    h_ref, g_ref, wg_ref, wu_ref = refs[:4]
    wd_refs = refs[4:4 + nb]
    gf_ref, o_ref, n_ref = refs[4 + nb:]
    j = pl.program_id(1)

    @pl.when(j == 0)
    def _():
        h = h_ref[...]
        n_ref[...] = _rms(h, g_ref[...]).astype(BF16)
        o_ref[...] = h

    n = n_ref[...]
    g = jnp.dot(n, wg_ref[...], preferred_element_type=F32)
    u = jnp.dot(n, wu_ref[...], preferred_element_type=F32)
    act = (g * _sigmoid(g) * u).astype(BF16)
    for c in range(nb):
        sl = slice(c * Z_BLK, (c + 1) * Z_BLK)
        o_ref[:, sl] += jnp.dot(act, wd_refs[c][...], preferred_element_type=F32)

    if final_norm:
        @pl.when(j == pl.num_programs(1) - 1)
        def _():
            o_ref[...] = _rms(o_ref[...], gf_ref[...])


def _ffn(h, g, w_gate_up, w_down, g_final, tm, tf, final_norm):
    s, d = h.shape
    dff = w_down.shape[0]
    nf = dff // tf
    return pl.pallas_call(
        functools.partial(_ffn_kernel, final_norm=final_norm),
        grid=(s // tm, nf),
        in_specs=[
            pl.BlockSpec((tm, d), lambda i, j: (i, 0)),
            pl.BlockSpec((1, d), lambda i, j: (0, 0)),
            pl.BlockSpec((d, tf), lambda i, j: (0, j)),
            pl.BlockSpec((d, tf), lambda i, j: (0, j + nf)),
        ]
        + [pl.BlockSpec((tf, Z_BLK), functools.partial(lambda i, j, c: (j, c), c=c))
           for c in range(N_GATE_BLKS)]
        + [pl.BlockSpec((1, d), lambda i, j: (0, 0))],
        out_specs=pl.BlockSpec((tm, d), lambda i, j: (i, 0)),
        out_shape=jax.ShapeDtypeStruct((s, d), F32),
        scratch_shapes=[pltpu.VMEM((tm, d), BF16)],
        compiler_params=pltpu.CompilerParams(
            dimension_semantics=("arbitrary", "arbitrary"), vmem_limit_bytes=VMEM_LIMIT),
        name="ffn",
    )(h, g.reshape(1, d), w_gate_up, w_gate_up, *([w_down] * N_GATE_BLKS), g_final.reshape(1, d))


def _rope_table(s):
    inv_freq = ROPE_THETA ** (-np.arange(0, ROT_DIM, 2, dtype=np.float64) / ROT_DIM)
    ang = np.arange(s, dtype=np.float64)[:, None] * inv_freq[None, :]
    cos, sin = np.cos(ang), np.sin(ang)
    ones = np.ones((s, HEAD_DIM - ROT_DIM))
    zeros = np.zeros((s, HEAD_DIM - ROT_DIM))
    return jnp.asarray(np.concatenate([cos, cos, ones, -sin, sin, zeros], axis=1), dtype=F32)


def kernel(x, mem, g_mix, w_in, conv_w, sinks, w_attn_o, w_conv_o, w_out, g_xattn, g_mem,
           w_xq, w_xkv, w_xo, g_ffn, w_gate_up, w_down, g_final):
    batch, s, d = x.shape
    depth = w_in.shape[0]
    if depth < 1:
        raise ValueError("depth must be positive")
    tab = _rope_table(s)

    outs = []
    for b in range(batch):
        h = x[b]
        for l in range(depth):
            a, cv, zg, wao, wco, wout, wxq, wxo = _in_mix(
                h, g_mix[l], w_in[l].astype(BF16), tab, sinks[l], conv_w[l],
                (w_attn_o[l], w_conv_o[l], w_out[l], w_xq[l], w_xo[l]), tm=256)
            kt_mem, v_mem = _mem_kv(mem[b], g_mem[l], w_xkv[l])
            h, wgu, wdn = _mix_out(a, cv, zg, h, wao, wco, wout, g_xattn[l], wxq, kt_mem, v_mem, wxo,
                                   w_gate_up[l], w_down[l], tm=256)
            h = _ffn(h, g_ffn[l], wgu, wdn, g_final, tm=1024, tf=512, final_norm=(l == depth - 1))
        outs.append(h)
    return jnp.stack(outs, axis=0)
```

```python
import functools

import jax
import jax.numpy as jnp
import numpy as np
from jax import lax
from jax.experimental import pallas as pl
from jax.experimental.pallas import tpu as pltpu

F32 = jnp.float32
BF16 = jnp.bfloat16

EPS = 1e-6
HEAD_DIM = 64
N_Q_HEADS = 16
N_KV_HEADS = 4
Q_PER_KV = N_Q_HEADS // N_KV_HEADS
BLOCK = 128
ROT_DIM = HEAD_DIM // 4
ROPE_THETA = 500000.0
CONV_K = 3
X_HEADS = 4
X_HEAD_DIM = 128
LANES = 128
SUBLANES = 8

VMEM_LIMIT = 56 * 1024 * 1024


def _rms(x, g):
    ms = jnp.mean(x * x, axis=-1, keepdims=True)
    return x * lax.rsqrt(ms + EPS) * g


def _sigmoid(x):
    return 0.5 * jnp.tanh(0.5 * x) + 0.5


def _const_spec(shape):
    return pl.BlockSpec(shape, lambda i: (0,) * len(shape), pipeline_mode=pl.Buffered(1))


def _mem_kv_kernel(mem_ref, g_ref, w_ref, kt_ref, v_ref):
    n = _rms(mem_ref[...], g_ref[...]).astype(BF16)
    kv = jnp.dot(n, w_ref[...].astype(BF16), preferred_element_type=F32)
    xw = v_ref.shape[1]
    kt_ref[...] = kv[:, :xw].T.astype(BF16)
    v_ref[...] = kv[:, xw:].astype(BF16)


def _mem_kv(mem, g_mem, w_xkv):
    n_mem, d = mem.shape
    xw = w_xkv.shape[1] // 2
    return pl.pallas_call(
        _mem_kv_kernel,
        out_shape=[jax.ShapeDtypeStruct((xw, n_mem), BF16), jax.ShapeDtypeStruct((n_mem, xw), BF16)],
        compiler_params=pltpu.CompilerParams(vmem_limit_bytes=VMEM_LIMIT),
        name="mem_kv",
    )(mem, g_mem.reshape(1, d), w_xkv)


Z_BLK = 512
CH_Q, CH_KV, CH_CB, CH_CC, CH_CH, CH_GATE, CH_END = 0, 2, 3, 5, 7, 9, 17
N_GATE_BLKS = 4


def _in_mix_kernel(*refs, n_cast):
    x_ref, g_ref, w_ref, tab_ref, sink_ref, convw_ref = refs[:6]
    cast_in = refs[6:6 + n_cast]
    a_ref, cv_ref, zg_ref = refs[6 + n_cast:9 + n_cast]
    cast_out = refs[9 + n_cast:9 + 2 * n_cast]
    qbuf, kbuf, vbuf, vwin, cbuf, cbb = refs[9 + 2 * n_cast:]
    i = pl.program_id(0)
    tm = x_ref.shape[0]
    nblk = tm // BLOCK
    kvw = N_KV_HEADS * HEAD_DIM

    @pl.when(i == 0)
    def _():
        kbuf[0:BLOCK, :] = jnp.zeros((BLOCK, kvw), BF16)
        vbuf[:, 0:BLOCK] = jnp.zeros((kvw, BLOCK), BF16)
        cbuf[0:SUBLANES, :] = jnp.zeros((SUBLANES, cbuf.shape[1]), F32)

    @pl.when(i > 0)
    def _():
        kbuf[0:BLOCK, :] = kbuf[tm:tm + BLOCK, :]
        vbuf[:, 0:BLOCK] = vbuf[:, tm:tm + BLOCK]
        cbuf[0:SUBLANES, :] = cbuf[tm:tm + SUBLANES, :]

    n = _rms(x_ref[...], g_ref[...]).astype(BF16)

    def chunk(c):
        return jnp.dot(n, w_ref[:, c * Z_BLK:(c + 1) * Z_BLK], preferred_element_type=F32)

    half = ROT_DIM // 2
    tab = tab_ref[...]
    swapped = pltpu.roll(tab, HEAD_DIM, 1)
    lane = lax.broadcasted_iota(jnp.int32, tab.shape, 1)
    low = lane < HEAD_DIM
    hl = lane & (HEAD_DIM - 1)
    cos = jnp.where(low, tab, swapped)
    ss = jnp.where(low, swapped, tab)
    sa = jnp.where(hl < half, ss, 0.0)
    sb = jnp.where(hl >= half, ss, 0.0)

    def rope(t):
        return t * cos + pltpu.roll(t, LANES - half, 1) * sa + pltpu.roll(t, half, 1) * sb

    scale = HEAD_DIM ** -0.5

    def q_chunk(c):
        r = chunk(c)
        for g in range(Z_BLK // LANES):
            col = (c - CH_Q) * Z_BLK + g * LANES
            qbuf[col:col + LANES, :] = (rope(r[:, g * LANES:(g + 1) * LANES]) * scale).T.astype(BF16)

    def kv_chunk():
        r = chunk(CH_KV)
        for g in range(kvw // LANES):
            sl = slice(g * LANES, (g + 1) * LANES)
            kbuf[BLOCK:, sl] = rope(r[:, sl]).astype(BF16)
            vbuf[sl, BLOCK:] = r[:, kvw + g * LANES:kvw + (g + 1) * LANES].T.astype(BF16)
        lane = lax.broadcasted_iota(jnp.int32, (kvw, BLOCK), 1)
        for b in range(nblk):
            first = vbuf[:, b * BLOCK:(b + 1) * BLOCK].astype(F32)
            vwin[b, :, 0:BLOCK] = jnp.where(lane == 0, 0.0, first).astype(BF16)
            vwin[b, :, BLOCK:] = vbuf[:, (b + 1) * BLOCK:(b + 2) * BLOCK]

    def cb_chunk(c):
        cbb[:, (c - CH_CB) * Z_BLK:(c - CH_CB + 1) * Z_BLK] = chunk(c)

    def cc_chunk(c):
        cbuf[SUBLANES:, (c - CH_CC) * Z_BLK:(c - CH_CC + 1) * Z_BLK] = chunk(c)

    def ch_chunk(c):
        sl = slice((c - CH_CH) * Z_BLK, (c - CH_CH + 1) * Z_BLK)
        cbuf[SUBLANES:, sl] = cbuf[SUBLANES:, sl] * chunk(c)

    def gate_chunk(c):
        zg_ref[:, (c - CH_GATE) * Z_BLK:(c - CH_GATE + 1) * Z_BLK] = _sigmoid(chunk(c)).astype(BF16)

    cw = convw_ref[...]
    conv_cols = 2 * LANES

    def conv_chunk(c):
        sl = slice(c * conv_cols, (c + 1) * conv_cols)
        y = cw[2:3, sl] * cbuf[SUBLANES:, sl]
        y = y + cw[1:2, sl] * cbuf[SUBLANES - 1:SUBLANES - 1 + tm, sl]
        y = y + cw[0:1, sl] * cbuf[SUBLANES - 2:SUBLANES - 2 + tm, sl]
        cv_ref[:, sl] = (cbb[:, sl] * y).astype(BF16)

    pair_w = 2 * BLOCK
    ci = lax.broadcasted_iota(jnp.int32, (2 * BLOCK, pair_w), 0)
    qi = lax.broadcasted_iota(jnp.int32, (2 * BLOCK, pair_w), 1) & (BLOCK - 1)
    rel = qi + BLOCK - ci
    band = (rel >= 0) & (rel < BLOCK)
    band_first = band & (ci >= jnp.where(i == 0, BLOCK, 0))
    top_row = lax.broadcasted_iota(jnp.int32, (SUBLANES, pair_w), 0) == 0
    top_second = lax.broadcasted_iota(jnp.int32, (SUBLANES, pair_w), 1) >= BLOCK

    def attn_task(b, h0):
        g = h0 // Q_PER_KV
        rows = slice(b * BLOCK, (b + 1) * BLOCK)
        gcols = slice(g * HEAD_DIM, (g + 1) * HEAD_DIM)
        qcat = jnp.concatenate([qbuf[h * HEAD_DIM:(h + 1) * HEAD_DIM, rows] for h in (h0, h0 + 1)], axis=1)
        s = jnp.dot(kbuf[b * BLOCK:(b + 2) * BLOCK, gcols], qcat, preferred_element_type=F32)
        yield
        s = jnp.where(band_first if b == 0 else band, s, -jnp.inf)
        sink = jnp.where(top_second, sink_ref[h0 + 1], sink_ref[h0])
        s = jnp.concatenate([jnp.where(top_row, sink, s[0:SUBLANES]), s[SUBLANES:]], axis=0)
        m = jnp.max(s, axis=0, keepdims=True)
        p = jnp.exp(s - m)
        denom = jnp.sum(p, axis=0, keepdims=True)
        w = p.astype(BF16)
        yield
        o = jnp.dot(vwin[b, gcols, :], w, preferred_element_type=F32) * (1.0 / denom)
        both = jnp.concatenate([o[:, :BLOCK], o[:, BLOCK:]], axis=0)
        a_ref[rows, h0 * HEAD_DIM:(h0 + 2) * HEAD_DIM] = both.T.astype(BF16)
        yield

    tasks = [attn_task(b, h0) for b in range(nblk) for h0 in range(0, N_Q_HEADS, 2)]
    depth = 3
    n_ticks = len(tasks) + depth - 1
    ticks = iter(range(n_ticks))

    def run_ticks(count):
        for _ in range(count):
            t = next(ticks)
            for k in range(max(0, t - depth + 1), min(t + 1, len(tasks))):
                next(tasks[k])

    q_chunk(CH_Q)
    kv_chunk()
    rest = ([functools.partial(q_chunk, c) for c in range(CH_Q + 1, CH_KV)]
            + [functools.partial(cb_chunk, c) for c in range(CH_CB, CH_CC)]
            + [functools.partial(cc_chunk, c) for c in range(CH_CC, CH_CH)]
            + [functools.partial(ch_chunk, c) for c in range(CH_CH, CH_GATE)]
            + [functools.partial(gate_chunk, c) for c in range(CH_GATE, CH_END)])
    n_conv = cbuf.shape[1] // conv_cols
    span = len(rest) - 2
    for k, thunk in enumerate(rest):
        if k < span:
            run_ticks((k + 1) * n_ticks // span - k * n_ticks // span)
        thunk()
        kc = k - (CH_GATE - CH_CB) - (CH_KV - CH_Q - 1)
        if 0 <= kc < n_conv:
            conv_chunk(kc)
    assert next(ticks, None) is None
    for src, dst in zip(cast_in, cast_out):
        dst[...] = src[...].astype(BF16)


def _cast_spec(w, n_steps):
    rows = w.shape[0]
    pack = 2 * SUBLANES
    blk = pack
    while blk * n_steps < rows or rows % blk or n_steps % (rows // blk):
        blk += pack
    stride = n_steps // (rows // blk)
    return pl.BlockSpec((blk, w.shape[1]), lambda i: (jnp.minimum(i, n_steps - 1) // stride, 0))


def _in_mix(x, g, w, tab, sinks, conv_w, cast_ws, tm):
    s, d = x.shape
    cast_specs = [_cast_spec(cw_, s // tm) for cw_ in cast_ws]
    n = w.shape[1]
    aw = N_Q_HEADS * HEAD_DIM
    kvw = N_KV_HEADS * HEAD_DIM
    cw = conv_w.shape[1]
    gw = n - CH_GATE * Z_BLK
    assert aw == (CH_KV - CH_Q) * Z_BLK and 2 * kvw == Z_BLK and cw == (CH_CC - CH_CB) * Z_BLK
    assert n == CH_END * Z_BLK and gw == 2 * d
    return pl.pallas_call(
        functools.partial(_in_mix_kernel, n_cast=len(cast_ws)),
        grid=(s // tm,),
        in_specs=[
            pl.BlockSpec((tm, d), lambda i: (i, 0)),
            _const_spec((1, d)),
            _const_spec((d, n)),
            pl.BlockSpec((tm, LANES), lambda i: (i, 0)),
            pl.BlockSpec(memory_space=pltpu.SMEM),
            _const_spec((CONV_K, cw)),
        ] + cast_specs,
        out_specs=[
            pl.BlockSpec((tm, aw), lambda i: (i, 0)),
            pl.BlockSpec((tm, cw), lambda i: (i, 0)),
            pl.BlockSpec((tm, gw), lambda i: (i, 0)),
        ] + cast_specs,
        out_shape=[
            jax.ShapeDtypeStruct((s, aw), BF16),
            jax.ShapeDtypeStruct((s, cw), BF16),
            jax.ShapeDtypeStruct((s, gw), BF16),
        ] + [jax.ShapeDtypeStruct(cw_.shape, BF16) for cw_ in cast_ws],
        scratch_shapes=[
            pltpu.VMEM((aw, tm), BF16),
            pltpu.VMEM((tm + BLOCK, kvw), BF16),
            pltpu.VMEM((kvw, tm + BLOCK), BF16),
            pltpu.VMEM((tm // BLOCK, kvw, 2 * BLOCK), BF16),
            pltpu.VMEM((tm + SUBLANES, cw), F32),
            pltpu.VMEM((tm, cw), F32),
        ],
        compiler_params=pltpu.CompilerParams(
            dimension_semantics=("arbitrary",), vmem_limit_bytes=VMEM_LIMIT),
        name="in_mix",
    )(x, g.reshape(1, d), w, tab, sinks, conv_w, *cast_ws)


def _mix_out_kernel(*refs):
    nb = N_GATE_BLKS
    a_ref, cv_ref, zg_ref, x_ref = refs[:4]
    wao_refs, wco_refs, wout_refs = (refs[4 + k * nb:4 + (k + 1) * nb] for k in range(3))
    gx_ref, wxq_ref, kt_ref, vm_ref = refs[4 + 3 * nb:8 + 3 * nb]
    wxo_refs = refs[8 + 3 * nb:8 + 4 * nb]
    wgu_in, wd_in, o_ref, wgu_out, wd_out, mbuf, h1buf, obuf = refs[8 + 4 * nb:]
    d = x_ref.shape[1]
    i = pl.program_id(0)
    h1_cur = h1buf.at[i % 2]
    h1_prev = h1buf.at[(i + 1) % 2]

    @pl.when(i == 0)
    def _():
        h1buf[1] = jnp.zeros(h1buf.shape[1:], F32)

    def merge_chunk(c):
        sl = slice(c * Z_BLK, (c + 1) * Z_BLK)
        ya = jnp.dot(a_ref[...], wao_refs[c][...], preferred_element_type=F32)
        yc = jnp.dot(cv_ref[...], wco_refs[c][...], preferred_element_type=F32)
        ga = zg_ref[:, sl].astype(F32)
        gc = zg_ref[:, d + c * Z_BLK:d + (c + 1) * Z_BLK].astype(F32)
        mbuf[:, sl] = (ga * ya + gc * yc).astype(BF16)

    def out_chunk(c):
        sl = slice(c * Z_BLK, (c + 1) * Z_BLK)
        h1_cur[:, sl] = x_ref[:, sl] + jnp.dot(mbuf[...], wout_refs[c][...], preferred_element_type=F32)

    state = {}

    def xq():
        nrm = _rms(h1_prev[...], gx_ref[...]).astype(BF16)
        q = jnp.dot(nrm, wxq_ref[...], preferred_element_type=F32) * (X_HEAD_DIM ** -0.5)
        state["q"] = q.astype(BF16)

    def xhead(hd):
        sl = slice(hd * X_HEAD_DIM, (hd + 1) * X_HEAD_DIM)
        s = jnp.dot(state["q"][:, sl], kt_ref[sl, :], preferred_element_type=F32)
        p = jnp.exp(s - jnp.max(s, axis=-1, keepdims=True))
        denom = jnp.sum(p, axis=-1, keepdims=True)
        o = jnp.dot(p.astype(BF16), vm_ref[:, sl], preferred_element_type=F32)
        obuf[:, sl] = (o * (1.0 / denom)).astype(BF16)

    def xout(c):
        sl = slice(c * Z_BLK, (c + 1) * Z_BLK)
        o_ref[:, sl] = h1_prev[:, sl] + jnp.dot(obuf[...], wxo_refs[c][...], preferred_element_type=F32)

    def cast_gu():
        wgu_out[...] = wgu_in[...].astype(BF16)

    def cast_d():
        wd_out[...] = wd_in[...].astype(BF16)

    part = functools.partial
    for thunk in (part(merge_chunk, 0), xq, cast_gu, part(merge_chunk, 1), part(xhead, 0),
                  part(merge_chunk, 2), part(xhead, 1), part(merge_chunk, 3), part(xhead, 2),
                  part(out_chunk, 0), part(xhead, 3), part(out_chunk, 1), part(xout, 0), cast_d,
                  part(out_chunk, 2), part(xout, 1), part(out_chunk, 3), part(xout, 2), part(xout, 3)):
        thunk()


def _mix_out(a, cv, zg, x, w_attn_o, w_conv_o, w_out, g_xattn, w_xq, kt_mem, v_mem, w_xo, w_gate_up, w_down,
             tm):
    s, d = x.shape
    n = s // tm
    xw = w_xq.shape[1]
    assert d == N_GATE_BLKS * Z_BLK

    def cur(width):
        return pl.BlockSpec((tm, width), lambda i: (jnp.minimum(i, n - 1), 0))

    def col_block(w, k):
        return pl.BlockSpec((w.shape[0], Z_BLK), lambda i: (0, k), pipeline_mode=pl.Buffered(1))

    gu_spec = _cast_spec(w_gate_up, n)
    d_spec = _cast_spec(w_down, n)
    ws = (w_attn_o, w_conv_o, w_out)
    return pl.pallas_call(
        _mix_out_kernel,
        grid=(n + 1,),
        in_specs=[cur(a.shape[1]), cur(cv.shape[1]), cur(zg.shape[1]), cur(d)]
        + [col_block(w, k) for w in ws for k in range(N_GATE_BLKS)]
        + [_const_spec((1, d)), _const_spec((d, xw)), _const_spec(kt_mem.shape), _const_spec(v_mem.shape)]
        + [col_block(w_xo, k) for k in range(N_GATE_BLKS)]
        + [gu_spec, d_spec],
        out_specs=[pl.BlockSpec((tm, d), lambda i: (jnp.maximum(i - 1, 0), 0)), gu_spec, d_spec],
        out_shape=[
            jax.ShapeDtypeStruct((s, d), F32),
            jax.ShapeDtypeStruct(w_gate_up.shape, BF16),
            jax.ShapeDtypeStruct(w_down.shape, BF16),
        ],
        scratch_shapes=[
            pltpu.VMEM((tm, d), BF16),
            pltpu.VMEM((2, tm, d), F32),
            pltpu.VMEM((tm, xw), BF16),
        ],
        compiler_params=pltpu.CompilerParams(
            dimension_semantics=("arbitrary",), vmem_limit_bytes=VMEM_LIMIT),
        name="mix_out",
    )(a, cv, zg, x, *[w for w in ws for _ in range(N_GATE_BLKS)],
      g_xattn.reshape(1, d), w_xq, kt_mem, v_mem, *([w_xo] * N_GATE_BLKS), w_gate_up, w_down)


def _ffn_kernel(*refs, final_norm):
    nb = N_GATE_BLKS
    h_ref, g_ref, wg_ref, wu_ref = refs[:4]
    wd_refs = refs[4:4 + nb]
    gf_ref, o_ref, n_ref = refs[4 + nb:]
    j = pl.program_id(1)

    @pl.when(j == 0)
    def _():
        h = h_ref[...]
        n_ref[...] = _rms(h, g_ref[...]).astype(BF16)
        o_ref[...] = h

    n = n_ref[...]
    g = jnp.dot(n, wg_ref[...], preferred_element_type=F32)
    u = jnp.dot(n, wu_ref[...], preferred_element_type=F32)
    act = (g * _sigmoid(g) * u).astype(BF16)
    for c in range(nb):
        sl = slice(c * Z_BLK, (c + 1) * Z_BLK)
        o_ref[:, sl] += jnp.dot(act, wd_refs[c][...], preferred_element_type=F32)

    if final_norm:
        @pl.when(j == pl.num_programs(1) - 1)
        def _():
            o_ref[...] = _rms(o_ref[...], gf_ref[...])


def _ffn(h, g, w_gate_up, w_down, g_final, tm, tf, final_norm):
    s, d = h.shape
    dff = w_down.shape[0]
    nf = dff // tf
    return pl.pallas_call(
        functools.partial(_ffn_kernel, final_norm=final_norm),
        grid=(s // tm, nf),
        in_specs=[
            pl.BlockSpec((tm, d), lambda i, j: (i, 0)),
            pl.BlockSpec((1, d), lambda i, j: (0, 0)),
            pl.BlockSpec((d, tf), lambda i, j: (0, j)),
            pl.BlockSpec((d, tf), lambda i, j: (0, j + nf)),
        ]
        + [pl.BlockSpec((tf, Z_BLK), functools.partial(lambda i, j, c: (j, c), c=c))
           for c in range(N_GATE_BLKS)]
        + [pl.BlockSpec((1, d), lambda i, j: (0, 0))],
        out_specs=pl.BlockSpec((tm, d), lambda i, j: (i, 0)),
        out_shape=jax.ShapeDtypeStruct((s, d), F32),
        scratch_shapes=[pltpu.VMEM((tm, d), BF16)],
        compiler_params=pltpu.CompilerParams(
            dimension_semantics=("arbitrary", "arbitrary"), vmem_limit_bytes=VMEM_LIMIT),
        name="ffn",
    )(h, g.reshape(1, d), w_gate_up, w_gate_up, *([w_down] * N_GATE_BLKS), g_final.reshape(1, d))


def _rope_table(s):
    inv_freq = ROPE_THETA ** (-np.arange(0, ROT_DIM, 2, dtype=np.float64) / ROT_DIM)
    ang = np.arange(s, dtype=np.float64)[:, None] * inv_freq[None, :]
    cos, sin = np.cos(ang), np.sin(ang)
    ones = np.ones((s, HEAD_DIM - ROT_DIM))
    zeros = np.zeros((s, HEAD_DIM - ROT_DIM))
    return jnp.asarray(np.concatenate([cos, cos, ones, -sin, sin, zeros], axis=1), dtype=F32)


def kernel(x, mem, g_mix, w_in, conv_w, sinks, w_attn_o, w_conv_o, w_out, g_xattn, g_mem,
           w_xq, w_xkv, w_xo, g_ffn, w_gate_up, w_down, g_final):
    batch, s, d = x.shape
    depth = w_in.shape[0]
    if depth < 1:
        raise ValueError("depth must be positive")
    tab = _rope_table(s)

    outs = []
    for b in range(batch):
        h = x[b]
        for l in range(depth):
            a, cv, zg, wao, wco, wout, wxq, wxo = _in_mix(
                h, g_mix[l], w_in[l].astype(BF16), tab, sinks[l], conv_w[l],
                (w_attn_o[l], w_conv_o[l], w_out[l], w_xq[l], w_xo[l]), tm=256)
            kt_mem, v_mem = _mem_kv(mem[b], g_mem[l], w_xkv[l])
            h, wgu, wdn = _mix_out(a, cv, zg, h, wao, wco, wout, g_xattn[l], wxq, kt_mem, v_mem, wxo,
                                   w_gate_up[l], w_down[l], tm=256)
            h = _ffn(h, g_ffn[l], wgu, wdn, g_final, tm=1024, tf=512, final_norm=(l == depth - 1))
        outs.append(h)
    return jnp.stack(outs, axis=0)
```

```python
import functools

import jax
import jax.numpy as jnp
import numpy as np
from jax import lax
from jax.experimental import pallas as pl
from jax.experimental.pallas import tpu as pltpu

F32 = jnp.float32
BF16 = jnp.bfloat16

EPS = 1e-6
HEAD_DIM = 64
N_Q_HEADS = 16
N_KV_HEADS = 4
Q_PER_KV = N_Q_HEADS // N_KV_HEADS
BLOCK = 128
ROT_DIM = HEAD_DIM // 4
ROPE_THETA = 500000.0
CONV_K = 3
X_HEADS = 4
X_HEAD_DIM = 128
LANES = 128
SUBLANES = 8

VMEM_LIMIT = 56 * 1024 * 1024


def _rms(x, g):
    ms = jnp.mean(x * x, axis=-1, keepdims=True)
    return x * lax.rsqrt(ms + EPS) * g


def _sigmoid(x):
    return 0.5 * jnp.tanh(0.5 * x) + 0.5


def _const_spec(shape):
    return pl.BlockSpec(shape, lambda i: (0,) * len(shape), pipeline_mode=pl.Buffered(1))


def _mem_kv_kernel(mem_ref, g_ref, w_ref, kt_ref, v_ref):
    n = _rms(mem_ref[...], g_ref[...]).astype(BF16)
    kv = jnp.dot(n, w_ref[...].astype(BF16), preferred_element_type=F32)
    xw = v_ref.shape[1]
    kt_ref[...] = kv[:, :xw].T.astype(BF16)
    v_ref[...] = kv[:, xw:].astype(BF16)


def _mem_kv(mem, g_mem, w_xkv):
    n_mem, d = mem.shape
    xw = w_xkv.shape[1] // 2
    return pl.pallas_call(
        _mem_kv_kernel,
        out_shape=[jax.ShapeDtypeStruct((xw, n_mem), BF16), jax.ShapeDtypeStruct((n_mem, xw), BF16)],
        compiler_params=pltpu.CompilerParams(vmem_limit_bytes=VMEM_LIMIT),
        name="mem_kv",
    )(mem, g_mem.reshape(1, d), w_xkv)


Z_BLK = 512
CH_Q, CH_KV, CH_CB, CH_CC, CH_CH, CH_GATE, CH_END = 0, 2, 3, 5, 7, 9, 17
N_GATE_BLKS = 4


def _in_mix_kernel(*refs, n_cast):
    x_ref, g_ref, w_ref, tab_ref, sink_ref, convw_ref = refs[:6]
    cast_in = refs[6:6 + n_cast]
    a_ref, cv_ref, zg_ref = refs[6 + n_cast:9 + n_cast]
    cast_out = refs[9 + n_cast:9 + 2 * n_cast]
    qbuf, kbuf, vbuf, vwin, cbuf, cbb = refs[9 + 2 * n_cast:]
    i = pl.program_id(0)
    tm = x_ref.shape[0]
    nblk = tm // BLOCK
    kvw = N_KV_HEADS * HEAD_DIM

    @pl.when(i == 0)
    def _():
        kbuf[0:BLOCK, :] = jnp.zeros((BLOCK, kvw), BF16)
        vbuf[:, 0:BLOCK] = jnp.zeros((kvw, BLOCK), BF16)
        cbuf[0:SUBLANES, :] = jnp.zeros((SUBLANES, cbuf.shape[1]), F32)

    @pl.when(i > 0)
    def _():
        kbuf[0:BLOCK, :] = kbuf[tm:tm + BLOCK, :]
        vbuf[:, 0:BLOCK] = vbuf[:, tm:tm + BLOCK]
        cbuf[0:SUBLANES, :] = cbuf[tm:tm + SUBLANES, :]

    n = _rms(x_ref[...], g_ref[...]).astype(BF16)

    def chunk(c):
        return jnp.dot(n, w_ref[:, c * Z_BLK:(c + 1) * Z_BLK], preferred_element_type=F32)

    half = ROT_DIM // 2
    tab = tab_ref[...]
    swapped = pltpu.roll(tab, HEAD_DIM, 1)
    lane = lax.broadcasted_iota(jnp.int32, tab.shape, 1)
    low = lane < HEAD_DIM
    hl = lane & (HEAD_DIM - 1)
    cos = jnp.where(low, tab, swapped)
    ss = jnp.where(low, swapped, tab)
    sa = jnp.where(hl < half, ss, 0.0)
    sb = jnp.where(hl >= half, ss, 0.0)

    def rope(t):
        return t * cos + pltpu.roll(t, LANES - half, 1) * sa + pltpu.roll(t, half, 1) * sb

    scale = HEAD_DIM ** -0.5

    def q_chunk(c):
        r = chunk(c)
        for g in range(Z_BLK // LANES):
            col = (c - CH_Q) * Z_BLK + g * LANES
            qbuf[col:col + LANES, :] = (rope(r[:, g * LANES:(g + 1) * LANES]) * scale).T.astype(BF16)

    def kv_chunk():
        r = chunk(CH_KV)
        for g in range(kvw // LANES):
            sl = slice(g * LANES, (g + 1) * LANES)
            kbuf[BLOCK:, sl] = rope(r[:, sl]).astype(BF16)
            vbuf[sl, BLOCK:] = r[:, kvw + g * LANES:kvw + (g + 1) * LANES].T.astype(BF16)
        lane = lax.broadcasted_iota(jnp.int32, (kvw, BLOCK), 1)
        for b in range(nblk):
            first = vbuf[:, b * BLOCK:(b + 1) * BLOCK].astype(F32)
            vwin[b, :, 0:BLOCK] = jnp.where(lane == 0, 0.0, first).astype(BF16)
            vwin[b, :, BLOCK:] = vbuf[:, (b + 1) * BLOCK:(b + 2) * BLOCK]

    def cb_chunk(c):
        cbb[:, (c - CH_CB) * Z_BLK:(c - CH_CB + 1) * Z_BLK] = chunk(c)

    def cc_chunk(c):
        cbuf[SUBLANES:, (c - CH_CC) * Z_BLK:(c - CH_CC + 1) * Z_BLK] = chunk(c)

    def ch_chunk(c):
        sl = slice((c - CH_CH) * Z_BLK, (c - CH_CH + 1) * Z_BLK)
        cbuf[SUBLANES:, sl] = cbuf[SUBLANES:, sl] * chunk(c)

    def gate_chunk(c):
        zg_ref[:, (c - CH_GATE) * Z_BLK:(c - CH_GATE + 1) * Z_BLK] = _sigmoid(chunk(c)).astype(BF16)

    cw = convw_ref[...]
    conv_cols = 2 * LANES

    def conv_chunk(c):
        sl = slice(c * conv_cols, (c + 1) * conv_cols)
        y = cw[2:3, sl] * cbuf[SUBLANES:, sl]
        y = y + cw[1:2, sl] * cbuf[SUBLANES - 1:SUBLANES - 1 + tm, sl]
        y = y + cw[0:1, sl] * cbuf[SUBLANES - 2:SUBLANES - 2 + tm, sl]
        cv_ref[:, sl] = (cbb[:, sl] * y).astype(BF16)

    pair_w = 2 * BLOCK
    ci = lax.broadcasted_iota(jnp.int32, (2 * BLOCK, pair_w), 0)
    qi = lax.broadcasted_iota(jnp.int32, (2 * BLOCK, pair_w), 1) & (BLOCK - 1)
    rel = qi + BLOCK - ci
    band = (rel >= 0) & (rel < BLOCK)
    band_first = band & (ci >= jnp.where(i == 0, BLOCK, 0))
    top_row = lax.broadcasted_iota(jnp.int32, (SUBLANES, pair_w), 0) == 0
    top_second = lax.broadcasted_iota(jnp.int32, (SUBLANES, pair_w), 1) >= BLOCK

    def attn_task(b, h0):
        g = h0 // Q_PER_KV
        rows = slice(b * BLOCK, (b + 1) * BLOCK)
        gcols = slice(g * HEAD_DIM, (g + 1) * HEAD_DIM)
        qcat = jnp.concatenate([qbuf[h * HEAD_DIM:(h + 1) * HEAD_DIM, rows] for h in (h0, h0 + 1)], axis=1)
        s = jnp.dot(kbuf[b * BLOCK:(b + 2) * BLOCK, gcols], qcat, preferred_element_type=F32)
        yield
        s = jnp.where(band_first if b == 0 else band, s, -jnp.inf)
        sink = jnp.where(top_second, sink_ref[h0 + 1], sink_ref[h0])
        s = jnp.concatenate([jnp.where(top_row, sink, s[0:SUBLANES]), s[SUBLANES:]], axis=0)
        m = jnp.max(s, axis=0, keepdims=True)
        p = jnp.exp(s - m)
        denom = jnp.sum(p, axis=0, keepdims=True)
        w = p.astype(BF16)
        yield
        o = jnp.dot(vwin[b, gcols, :], w, preferred_element_type=F32) * (1.0 / denom)
        both = jnp.concatenate([o[:, :BLOCK], o[:, BLOCK:]], axis=0)
        a_ref[rows, h0 * HEAD_DIM:(h0 + 2) * HEAD_DIM] = both.T.astype(BF16)
        yield

    tasks = [attn_task(b, h0) for b in range(nblk) for h0 in range(0, N_Q_HEADS, 2)]
    depth = 3
    n_ticks = len(tasks) + depth - 1
    ticks = iter(range(n_ticks))

    def run_ticks(count):
        for _ in range(count):
            t = next(ticks)
            for k in range(max(0, t - depth + 1), min(t + 1, len(tasks))):
                next(tasks[k])

    q_chunk(CH_Q)
    kv_chunk()
    rest = ([functools.partial(q_chunk, c) for c in range(CH_Q + 1, CH_KV)]
            + [functools.partial(cb_chunk, c) for c in range(CH_CB, CH_CC)]
            + [functools.partial(cc_chunk, c) for c in range(CH_CC, CH_CH)]
            + [functools.partial(ch_chunk, c) for c in range(CH_CH, CH_GATE)]
            + [functools.partial(gate_chunk, c) for c in range(CH_GATE, CH_END)])
    n_conv = cbuf.shape[1] // conv_cols
    span = len(rest) - 2
    for k, thunk in enumerate(rest):
        if k < span:
            run_ticks((k + 1) * n_ticks // span - k * n_ticks // span)
        thunk()
        kc = k - (CH_GATE - CH_CB) - (CH_KV - CH_Q - 1)
        if 0 <= kc < n_conv:
            conv_chunk(kc)
    assert next(ticks, None) is None
    for src, dst in zip(cast_in, cast_out):
        dst[...] = src[...].astype(BF16)


def _cast_spec(w, n_steps):
    rows = w.shape[0]
    pack = 2 * SUBLANES
    blk = pack
    while blk * n_steps < rows or rows % blk or n_steps % (rows // blk):
        blk += pack
    stride = n_steps // (rows // blk)
    return pl.BlockSpec((blk, w.shape[1]), lambda i: (jnp.minimum(i, n_steps - 1) // stride, 0))


def _in_mix(x, g, w, tab, sinks, conv_w, cast_ws, tm):
    s, d = x.shape
    cast_specs = [_cast_spec(cw_, s // tm) for cw_ in cast_ws]
    n = w.shape[1]
    aw = N_Q_HEADS * HEAD_DIM
    kvw = N_KV_HEADS * HEAD_DIM
    cw = conv_w.shape[1]
    gw = n - CH_GATE * Z_BLK
    assert aw == (CH_KV - CH_Q) * Z_BLK and 2 * kvw == Z_BLK and cw == (CH_CC - CH_CB) * Z_BLK
    assert n == CH_END * Z_BLK and gw == 2 * d
    return pl.pallas_call(
        functools.partial(_in_mix_kernel, n_cast=len(cast_ws)),
        grid=(s // tm,),
        in_specs=[
            pl.BlockSpec((tm, d), lambda i: (i, 0)),
            _const_spec((1, d)),
            _const_spec((d, n)),
            pl.BlockSpec((tm, LANES), lambda i: (i, 0)),
            pl.BlockSpec(memory_space=pltpu.SMEM),
            _const_spec((CONV_K, cw)),
        ] + cast_specs,
        out_specs=[
            pl.BlockSpec((tm, aw), lambda i: (i, 0)),
            pl.BlockSpec((tm, cw), lambda i: (i, 0)),
            pl.BlockSpec((tm, gw), lambda i: (i, 0)),
        ] + cast_specs,
        out_shape=[
            jax.ShapeDtypeStruct((s, aw), BF16),
            jax.ShapeDtypeStruct((s, cw), BF16),
            jax.ShapeDtypeStruct((s, gw), BF16),
        ] + [jax.ShapeDtypeStruct(cw_.shape, BF16) for cw_ in cast_ws],
        scratch_shapes=[
            pltpu.VMEM((aw, tm), BF16),
            pltpu.VMEM((tm + BLOCK, kvw), BF16),
            pltpu.VMEM((kvw, tm + BLOCK), BF16),
            pltpu.VMEM((tm // BLOCK, kvw, 2 * BLOCK), BF16),
            pltpu.VMEM((tm + SUBLANES, cw), F32),
            pltpu.VMEM((tm, cw), F32),
        ],
        compiler_params=pltpu.CompilerParams(
            dimension_semantics=("arbitrary",), vmem_limit_bytes=VMEM_LIMIT),
        name="in_mix",
    )(x, g.reshape(1, d), w, tab, sinks, conv_w, *cast_ws)


def _mix_out_kernel(*refs):
    nb = N_GATE_BLKS
    a_ref, cv_ref, zg_ref, x_ref = refs[:4]
    wao_refs, wco_refs, wout_refs = (refs[4 + k * nb:4 + (k + 1) * nb] for k in range(3))
    gx_ref, wxq_ref, kt_ref, vm_ref = refs[4 + 3 * nb:8 + 3 * nb]
    wxo_refs = refs[8 + 3 * nb:8 + 4 * nb]
    wgu_in, wd_in, o_ref, wgu_out, wd_out, mbuf, h1buf, obuf = refs[8 + 4 * nb:]
    d = x_ref.shape[1]
    i = pl.program_id(0)
    h1_cur = h1buf.at[i % 2]
    h1_prev = h1buf.at[(i + 1) % 2]

    @pl.when(i == 0)
    def _():
        h1buf[1] = jnp.zeros(h1buf.shape[1:], F32)

    def merge_chunk(c):
        sl = slice(c * Z_BLK, (c + 1) * Z_BLK)
        ya = jnp.dot(a_ref[...], wao_refs[c][...], preferred_element_type=F32)
        yc = jnp.dot(cv_ref[...], wco_refs[c][...], preferred_element_type=F32)
        ga = zg_ref[:, sl].astype(F32)
        gc = zg_ref[:, d + c * Z_BLK:d + (c + 1) * Z_BLK].astype(F32)
        mbuf[:, sl] = (ga * ya + gc * yc).astype(BF16)

    def out_chunk(c):
        sl = slice(c * Z_BLK, (c + 1) * Z_BLK)
        h1_cur[:, sl] = x_ref[:, sl] + jnp.dot(mbuf[...], wout_refs[c][...], preferred_element_type=F32)

    state = {}

    def xnorm():
        state["n"] = _rms(h1_prev[...], gx_ref[...]).astype(BF16)

    def xq():
        q = jnp.dot(state.pop("n"), wxq_ref[...], preferred_element_type=F32) * (X_HEAD_DIM ** -0.5)
        state["q"] = q.astype(BF16)

    def xhead(hd):
        sl = slice(hd * X_HEAD_DIM, (hd + 1) * X_HEAD_DIM)
        s = jnp.dot(state["q"][:, sl], kt_ref[sl, :], preferred_element_type=F32)
        yield
        p = jnp.exp(s - jnp.max(s, axis=-1, keepdims=True))
        denom = jnp.sum(p, axis=-1, keepdims=True)
        w = p.astype(BF16)
        yield
        o = jnp.dot(w, vm_ref[:, sl], preferred_element_type=F32)
        obuf[:, sl] = (o * (1.0 / denom)).astype(BF16)
        yield

    def xout(c):
        sl = slice(c * Z_BLK, (c + 1) * Z_BLK)
        o_ref[:, sl] = h1_prev[:, sl] + jnp.dot(obuf[...], wxo_refs[c][...], preferred_element_type=F32)

    def cast_gu():
        wgu_out[...] = wgu_in[...].astype(BF16)

    def cast_d():
        wd_out[...] = wd_in[...].astype(BF16)

    part = functools.partial
    heads = [xhead(hd) for hd in range(X_HEADS)]
    depth = 3
    ticks = iter(range(len(heads) + depth - 1))

    def tick():
        t = next(ticks)
        for k in range(max(0, t - depth + 1), min(t + 1, len(heads))):
            next(heads[k])

    for thunk in (xnorm, part(merge_chunk, 0), xq, cast_gu, tick, part(merge_chunk, 1), tick, part(merge_chunk, 2),
                  tick, part(merge_chunk, 3), tick, part(out_chunk, 0), tick, part(out_chunk, 1), tick,
                  cast_d, part(xout, 0), part(out_chunk, 2), part(xout, 1), part(xout, 2),
                  part(out_chunk, 3), part(xout, 3)):
        thunk()
    assert next(ticks, None) is None


def _mix_out(a, cv, zg, x, w_attn_o, w_conv_o, w_out, g_xattn, w_xq, kt_mem, v_mem, w_xo, w_gate_up, w_down,
             tm):
    s, d = x.shape
    n = s // tm
    xw = w_xq.shape[1]
    assert d == N_GATE_BLKS * Z_BLK

    def cur(width):
        return pl.BlockSpec((tm, width), lambda i: (jnp.minimum(i, n - 1), 0))

    def col_block(w, k):
        return pl.BlockSpec((w.shape[0], Z_BLK), lambda i: (0, k), pipeline_mode=pl.Buffered(1))

    gu_spec = _cast_spec(w_gate_up, n)
    d_spec = _cast_spec(w_down, n)
    ws = (w_attn_o, w_conv_o, w_out)
    return pl.pallas_call(
        _mix_out_kernel,
        grid=(n + 1,),
        in_specs=[cur(a.shape[1]), cur(cv.shape[1]), cur(zg.shape[1]), cur(d)]
        + [col_block(w, k) for w in ws for k in range(N_GATE_BLKS)]
        + [_const_spec((1, d)), _const_spec((d, xw)), _const_spec(kt_mem.shape), _const_spec(v_mem.shape)]
        + [col_block(w_xo, k) for k in range(N_GATE_BLKS)]
        + [gu_spec, d_spec],
        out_specs=[pl.BlockSpec((tm, d), lambda i: (jnp.maximum(i - 1, 0), 0)), gu_spec, d_spec],
        out_shape=[
            jax.ShapeDtypeStruct((s, d), F32),
            jax.ShapeDtypeStruct(w_gate_up.shape, BF16),
            jax.ShapeDtypeStruct(w_down.shape, BF16),
        ],
        scratch_shapes=[
            pltpu.VMEM((tm, d), BF16),
            pltpu.VMEM((2, tm, d), F32),
            pltpu.VMEM((tm, xw), BF16),
        ],
        compiler_params=pltpu.CompilerParams(
            dimension_semantics=("arbitrary",), vmem_limit_bytes=VMEM_LIMIT),
        name="mix_out",
    )(a, cv, zg, x, *[w for w in ws for _ in range(N_GATE_BLKS)],
      g_xattn.reshape(1, d), w_xq, kt_mem, v_mem, *([w_xo] * N_GATE_BLKS), w_gate_up, w_down)


def _ffn_kernel(*refs, final_norm):
    nb = N_GATE_BLKS
    h_ref, g_ref, wg_ref, wu_ref = refs[:4]
    wd_refs = refs[4:4 + nb]
    gf_ref, o_ref, n_ref = refs[4 + nb:]
    j = pl.program_id(1)

    @pl.when(j == 0)
    def _():
        h = h_ref[...]
        n_ref[...] = _rms(h, g_ref[...]).astype(BF16)
        o_ref[...] = h

    n = n_ref[...]
    g = jnp.dot(n, wg_ref[...], preferred_element_type=F32)
    u = jnp.dot(n, wu_ref[...], preferred_element_type=F32)
    act = (g * _sigmoid(g) * u).astype(BF16)
    for c in range(nb):
        sl = slice(c * Z_BLK, (c + 1) * Z_BLK)
        o_ref[:, sl] += jnp.dot(act, wd_refs[c][...], preferred_element_type=F32)

    if final_norm:
        @pl.when(j == pl.num_programs(1) - 1)
        def _():
            o_ref[...] = _rms(o_ref[...], gf_ref[...])


def _ffn(h, g, w_gate_up, w_down, g_final, tm, tf, final_norm):
    s, d = h.shape
    dff = w_down.shape[0]
    nf = dff // tf
    return pl.pallas_call(
        functools.partial(_ffn_kernel, final_norm=final_norm),
        grid=(s // tm, nf),
        in_specs=[
            pl.BlockSpec((tm, d), lambda i, j: (i, 0)),
            pl.BlockSpec((1, d), lambda i, j: (0, 0)),
            pl.BlockSpec((d, tf), lambda i, j: (0, j)),
            pl.BlockSpec((d, tf), lambda i, j: (0, j + nf)),
        ]
        + [pl.BlockSpec((tf, Z_BLK), functools.partial(lambda i, j, c: (j, c), c=c))
           for c in range(N_GATE_BLKS)]
        + [pl.BlockSpec((1, d), lambda i, j: (0, 0))],
        out_specs=pl.BlockSpec((tm, d), lambda i, j: (i, 0)),
        out_shape=jax.ShapeDtypeStruct((s, d), F32),
        scratch_shapes=[pltpu.VMEM((tm, d), BF16)],
        compiler_params=pltpu.CompilerParams(
            dimension_semantics=("arbitrary", "arbitrary"), vmem_limit_bytes=VMEM_LIMIT),
        name="ffn",
    )(h, g.reshape(1, d), w_gate_up, w_gate_up, *([w_down] * N_GATE_BLKS), g_final.reshape(1, d))


def _rope_table(s):
    inv_freq = ROPE_THETA ** (-np.arange(0, ROT_DIM, 2, dtype=np.float64) / ROT_DIM)
    ang = np.arange(s, dtype=np.float64)[:, None] * inv_freq[None, :]
    cos, sin = np.cos(ang), np.sin(ang)
    ones = np.ones((s, HEAD_DIM - ROT_DIM))
    zeros = np.zeros((s, HEAD_DIM - ROT_DIM))
    return jnp.asarray(np.concatenate([cos, cos, ones, -sin, sin, zeros], axis=1), dtype=F32)


def kernel(x, mem, g_mix, w_in, conv_w, sinks, w_attn_o, w_conv_o, w_out, g_xattn, g_mem,
           w_xq, w_xkv, w_xo, g_ffn, w_gate_up, w_down, g_final):
    batch, s, d = x.shape
    depth = w_in.shape[0]
    if depth < 1:
        raise ValueError("depth must be positive")
    tab = _rope_table(s)

    outs = []
    for b in range(batch):
        h = x[b]
        for l in range(depth):
            a, cv, zg, wao, wco, wout, wxq, wxo = _in_mix(
                h, g_mix[l], w_in[l].astype(BF16), tab, sinks[l], conv_w[l],
                (w_attn_o[l], w_conv_o[l], w_out[l], w_xq[l], w_xo[l]), tm=256)
            kt_mem, v_mem = _mem_kv(mem[b], g_mem[l], w_xkv[l])
            h, wgu, wdn = _mix_out(a, cv, zg, h, wao, wco, wout, g_xattn[l], wxq, kt_mem, v_mem, wxo,
                                   w_gate_up[l], w_down[l], tm=256)
            h = _ffn(h, g_ffn[l], wgu, wdn, g_final, tm=1024, tf=512, final_norm=(l == depth - 1))
        outs.append(h)
    return jnp.stack(outs, axis=0)
```

```python
import functools

import jax
import jax.numpy as jnp
import numpy as np
from jax import lax
from jax.experimental import pallas as pl
from jax.experimental.pallas import tpu as pltpu

F32 = jnp.float32
BF16 = jnp.bfloat16

EPS = 1e-6
HEAD_DIM = 64
N_Q_HEADS = 16
N_KV_HEADS = 4
Q_PER_KV = N_Q_HEADS // N_KV_HEADS
BLOCK = 128
ROT_DIM = HEAD_DIM // 4
ROPE_THETA = 500000.0
CONV_K = 3
X_HEADS = 4
X_HEAD_DIM = 128
LANES = 128
SUBLANES = 8

VMEM_LIMIT = 56 * 1024 * 1024


def _rms(x, g):
    ms = jnp.mean(x * x, axis=-1, keepdims=True)
    return x * lax.rsqrt(ms + EPS) * g


def _sigmoid(x):
    return 0.5 * jnp.tanh(0.5 * x) + 0.5


def _const_spec(shape):
    return pl.BlockSpec(shape, lambda i: (0,) * len(shape), pipeline_mode=pl.Buffered(1))


def _mem_kv_kernel(mem_ref, g_ref, w_ref, kt_ref, v_ref):
    n = _rms(mem_ref[...], g_ref[...]).astype(BF16)
    kv = jnp.dot(n, w_ref[...].astype(BF16), preferred_element_type=F32)
    xw = v_ref.shape[1]
    kt_ref[...] = kv[:, :xw].T.astype(BF16)
    v_ref[...] = kv[:, xw:].astype(BF16)


def _mem_kv(mem, g_mem, w_xkv):
    n_mem, d = mem.shape
    xw = w_xkv.shape[1] // 2
    return pl.pallas_call(
        _mem_kv_kernel,
        out_shape=[jax.ShapeDtypeStruct((xw, n_mem), BF16), jax.ShapeDtypeStruct((n_mem, xw), BF16)],
        compiler_params=pltpu.CompilerParams(vmem_limit_bytes=VMEM_LIMIT),
        name="mem_kv",
    )(mem, g_mem.reshape(1, d), w_xkv)


Z_BLK = 512
CH_Q, CH_KV, CH_CB, CH_CC, CH_CH, CH_GATE, CH_END = 0, 2, 3, 5, 7, 9, 17
N_GATE_BLKS = 4


def _in_mix_kernel(*refs, n_cast):
    x_ref, g_ref, w_ref, tab_ref, sink_ref, convw_ref = refs[:6]
    cast_in = refs[6:6 + n_cast]
    a_ref, cv_ref, zg_ref = refs[6 + n_cast:9 + n_cast]
    cast_out = refs[9 + n_cast:9 + 2 * n_cast]
    qbuf, kbuf, vbuf, vwin, cbuf, cbb = refs[9 + 2 * n_cast:]
    i = pl.program_id(0)
    tm = x_ref.shape[0]
    nblk = tm // BLOCK
    kvw = N_KV_HEADS * HEAD_DIM

    @pl.when(i == 0)
    def _():
        kbuf[0:BLOCK, :] = jnp.zeros((BLOCK, kvw), BF16)
        vbuf[:, 0:BLOCK] = jnp.zeros((kvw, BLOCK), BF16)
        cbuf[0:SUBLANES, :] = jnp.zeros((SUBLANES, cbuf.shape[1]), F32)

    @pl.when(i > 0)
    def _():
        kbuf[0:BLOCK, :] = kbuf[tm:tm + BLOCK, :]
        vbuf[:, 0:BLOCK] = vbuf[:, tm:tm + BLOCK]
        cbuf[0:SUBLANES, :] = cbuf[tm:tm + SUBLANES, :]

    n = _rms(x_ref[...], g_ref[...]).astype(BF16)

    def chunk(c):
        return jnp.dot(n, w_ref[:, c * Z_BLK:(c + 1) * Z_BLK], preferred_element_type=F32)

    half = ROT_DIM // 2
    tab = tab_ref[...]
    swapped = pltpu.roll(tab, HEAD_DIM, 1)
    lane = lax.broadcasted_iota(jnp.int32, tab.shape, 1)
    low = lane < HEAD_DIM
    hl = lane & (HEAD_DIM - 1)
    cos = jnp.where(low, tab, swapped)
    ss = jnp.where(low, swapped, tab)
    sa = jnp.where(hl < half, ss, 0.0)
    sb = jnp.where(hl >= half, ss, 0.0)

    def rope(t):
        return t * cos + pltpu.roll(t, LANES - half, 1) * sa + pltpu.roll(t, half, 1) * sb

    scale = HEAD_DIM ** -0.5

    def q_chunk(c):
        r = chunk(c)
        for g in range(Z_BLK // LANES):
            col = (c - CH_Q) * Z_BLK + g * LANES
            qbuf[col:col + LANES, :] = (rope(r[:, g * LANES:(g + 1) * LANES]) * scale).T.astype(BF16)

    def kv_chunk():
        r = chunk(CH_KV)
        for g in range(kvw // LANES):
            sl = slice(g * LANES, (g + 1) * LANES)
            kbuf[BLOCK:, sl] = rope(r[:, sl]).astype(BF16)
            vbuf[sl, BLOCK:] = r[:, kvw + g * LANES:kvw + (g + 1) * LANES].T.astype(BF16)
        lane = lax.broadcasted_iota(jnp.int32, (kvw, BLOCK), 1)
        for b in range(nblk):
            first = vbuf[:, b * BLOCK:(b + 1) * BLOCK].astype(F32)
            vwin[b, :, 0:BLOCK] = jnp.where(lane == 0, 0.0, first).astype(BF16)
            vwin[b, :, BLOCK:] = vbuf[:, (b + 1) * BLOCK:(b + 2) * BLOCK]

    def cb_chunk(c):
        cbb[:, (c - CH_CB) * Z_BLK:(c - CH_CB + 1) * Z_BLK] = chunk(c)

    def cc_chunk(c):
        cbuf[SUBLANES:, (c - CH_CC) * Z_BLK:(c - CH_CC + 1) * Z_BLK] = chunk(c)

    def ch_chunk(c):
        sl = slice((c - CH_CH) * Z_BLK, (c - CH_CH + 1) * Z_BLK)
        cbuf[SUBLANES:, sl] = cbuf[SUBLANES:, sl] * chunk(c)

    def gate_chunk(c):
        zg_ref[:, (c - CH_GATE) * Z_BLK:(c - CH_GATE + 1) * Z_BLK] = _sigmoid(chunk(c)).astype(BF16)

    cw = convw_ref[...]
    conv_cols = 2 * LANES

    def conv_chunk(c):
        sl = slice(c * conv_cols, (c + 1) * conv_cols)
        y = cw[2:3, sl] * cbuf[SUBLANES:, sl]
        y = y + cw[1:2, sl] * cbuf[SUBLANES - 1:SUBLANES - 1 + tm, sl]
        y = y + cw[0:1, sl] * cbuf[SUBLANES - 2:SUBLANES - 2 + tm, sl]
        cv_ref[:, sl] = (cbb[:, sl] * y).astype(BF16)

    pair_w = 2 * BLOCK
    ci = lax.broadcasted_iota(jnp.int32, (2 * BLOCK, pair_w), 0)
    qi = lax.broadcasted_iota(jnp.int32, (2 * BLOCK, pair_w), 1) & (BLOCK - 1)
    rel = qi + BLOCK - ci
    band = (rel >= 0) & (rel < BLOCK)
    band_first = band & (ci >= jnp.where(i == 0, BLOCK, 0))
    top_row = lax.broadcasted_iota(jnp.int32, (SUBLANES, pair_w), 0) == 0
    top_second = lax.broadcasted_iota(jnp.int32, (SUBLANES, pair_w), 1) >= BLOCK

    def attn_task(b, h0):
        g = h0 // Q_PER_KV
        rows = slice(b * BLOCK, (b + 1) * BLOCK)
        gcols = slice(g * HEAD_DIM, (g + 1) * HEAD_DIM)
        qcat = jnp.concatenate([qbuf[h * HEAD_DIM:(h + 1) * HEAD_DIM, rows] for h in (h0, h0 + 1)], axis=1)
        s = jnp.dot(kbuf[b * BLOCK:(b + 2) * BLOCK, gcols], qcat, preferred_element_type=F32)
        yield
        s = jnp.where(band_first if b == 0 else band, s, -jnp.inf)
        sink = jnp.where(top_second, sink_ref[h0 + 1], sink_ref[h0])
        s = jnp.concatenate([jnp.where(top_row, sink, s[0:SUBLANES]), s[SUBLANES:]], axis=0)
        m = jnp.max(s, axis=0, keepdims=True)
        p = jnp.exp(s - m)
        denom = jnp.sum(p, axis=0, keepdims=True)
        w = p.astype(BF16)
        yield
        o = jnp.dot(vwin[b, gcols, :], w, preferred_element_type=F32) * (1.0 / denom)
        both = jnp.concatenate([o[:, :BLOCK], o[:, BLOCK:]], axis=0)
        a_ref[rows, h0 * HEAD_DIM:(h0 + 2) * HEAD_DIM] = both.T.astype(BF16)
        yield

    tasks = [attn_task(b, h0) for b in range(nblk) for h0 in range(0, N_Q_HEADS, 2)]
    depth = 3
    n_ticks = len(tasks) + depth - 1
    ticks = iter(range(n_ticks))

    def run_ticks(count):
        for _ in range(count):
            t = next(ticks)
            for k in range(max(0, t - depth + 1), min(t + 1, len(tasks))):
                next(tasks[k])

    q_chunk(CH_Q)
    kv_chunk()
    rest = ([functools.partial(q_chunk, c) for c in range(CH_Q + 1, CH_KV)]
            + [functools.partial(cb_chunk, c) for c in range(CH_CB, CH_CC)]
            + [functools.partial(cc_chunk, c) for c in range(CH_CC, CH_CH)]
            + [functools.partial(ch_chunk, c) for c in range(CH_CH, CH_GATE)]
            + [functools.partial(gate_chunk, c) for c in range(CH_GATE, CH_END)])
    n_conv = cbuf.shape[1] // conv_cols
    span = len(rest) - 2
    for k, thunk in enumerate(rest):
        if k < span:
            run_ticks((k + 1) * n_ticks // span - k * n_ticks // span)
        thunk()
        kc = k - (CH_GATE - CH_CB) - (CH_KV - CH_Q - 1)
        if 0 <= kc < n_conv:
            conv_chunk(kc)
    assert next(ticks, None) is None
    for src, dst in zip(cast_in, cast_out):
        dst[...] = src[...].astype(BF16)


def _cast_spec(w, n_steps):
    rows = w.shape[0]
    pack = 2 * SUBLANES
    blk = pack
    while blk * n_steps < rows or rows % blk or n_steps % (rows // blk):
        blk += pack
    stride = n_steps // (rows // blk)
    return pl.BlockSpec((blk, w.shape[1]), lambda i: (jnp.minimum(i, n_steps - 1) // stride, 0))


def _in_mix(x, g, w, tab, sinks, conv_w, cast_ws, tm):
    s, d = x.shape
    cast_specs = [_cast_spec(cw_, s // tm) for cw_ in cast_ws]
    n = w.shape[1]
    aw = N_Q_HEADS * HEAD_DIM
    kvw = N_KV_HEADS * HEAD_DIM
    cw = conv_w.shape[1]
    gw = n - CH_GATE * Z_BLK
    assert aw == (CH_KV - CH_Q) * Z_BLK and 2 * kvw == Z_BLK and cw == (CH_CC - CH_CB) * Z_BLK
    assert n == CH_END * Z_BLK and gw == 2 * d
    return pl.pallas_call(
        functools.partial(_in_mix_kernel, n_cast=len(cast_ws)),
        grid=(s // tm,),
        in_specs=[
            pl.BlockSpec((tm, d), lambda i: (i, 0)),
            _const_spec((1, d)),
            _const_spec((d, n)),
            pl.BlockSpec((tm, LANES), lambda i: (i, 0)),
            pl.BlockSpec(memory_space=pltpu.SMEM),
            _const_spec((CONV_K, cw)),
        ] + cast_specs,
        out_specs=[
            pl.BlockSpec((tm, aw), lambda i: (i, 0)),
            pl.BlockSpec((tm, cw), lambda i: (i, 0)),
            pl.BlockSpec((tm, gw), lambda i: (i, 0)),
        ] + cast_specs,
        out_shape=[
            jax.ShapeDtypeStruct((s, aw), BF16),
            jax.ShapeDtypeStruct((s, cw), BF16),
            jax.ShapeDtypeStruct((s, gw), BF16),
        ] + [jax.ShapeDtypeStruct(cw_.shape, BF16) for cw_ in cast_ws],
        scratch_shapes=[
            pltpu.VMEM((aw, tm), BF16),
            pltpu.VMEM((tm + BLOCK, kvw), BF16),
            pltpu.VMEM((kvw, tm + BLOCK), BF16),
            pltpu.VMEM((tm // BLOCK, kvw, 2 * BLOCK), BF16),
            pltpu.VMEM((tm + SUBLANES, cw), F32),
            pltpu.VMEM((tm, cw), F32),
        ],
        compiler_params=pltpu.CompilerParams(
            dimension_semantics=("arbitrary",), vmem_limit_bytes=VMEM_LIMIT),
        name="in_mix",
    )(x, g.reshape(1, d), w, tab, sinks, conv_w, *cast_ws)


def _mix_out_kernel(*refs):
    nb = N_GATE_BLKS
    a_ref, cv_ref, zg_ref, x_ref = refs[:4]
    wao_refs, wco_refs, wout_refs = (refs[4 + k * nb:4 + (k + 1) * nb] for k in range(3))
    gx_ref, wxq_ref, kt_ref, vm_ref = refs[4 + 3 * nb:8 + 3 * nb]
    wxo_refs = refs[8 + 3 * nb:8 + 4 * nb]
    wgu_in, wd_in, o_ref, wgu_out, wd_out, mbuf, h1buf, obuf = refs[8 + 4 * nb:]
    d = x_ref.shape[1]
    i = pl.program_id(0)
    h1_cur = h1buf.at[i % 2]
    h1_prev = h1buf.at[(i + 1) % 2]

    @pl.when(i == 0)
    def _():
        h1buf[1] = jnp.zeros(h1buf.shape[1:], F32)

    def merge_chunk(c):
        sl = slice(c * Z_BLK, (c + 1) * Z_BLK)
        ya = jnp.dot(a_ref[...], wao_refs[c][...], preferred_element_type=F32)
        yc = jnp.dot(cv_ref[...], wco_refs[c][...], preferred_element_type=F32)
        ga = zg_ref[:, sl].astype(F32)
        gc = zg_ref[:, d + c * Z_BLK:d + (c + 1) * Z_BLK].astype(F32)
        mbuf[:, sl] = (ga * ya + gc * yc).astype(BF16)

    def out_chunk(c):
        sl = slice(c * Z_BLK, (c + 1) * Z_BLK)
        h1_cur[:, sl] = x_ref[:, sl] + jnp.dot(mbuf[...], wout_refs[c][...], preferred_element_type=F32)

    state = {}

    def xnorm():
        state["n"] = _rms(h1_prev[...], gx_ref[...]).astype(BF16)

    def xq():
        q = jnp.dot(state.pop("n"), wxq_ref[...], preferred_element_type=F32) * (X_HEAD_DIM ** -0.5)
        state["q"] = q.astype(BF16)

    def xhead(hd):
        sl = slice(hd * X_HEAD_DIM, (hd + 1) * X_HEAD_DIM)
        s = jnp.dot(state["q"][:, sl], kt_ref[sl, :], preferred_element_type=F32)
        yield
        p = jnp.exp(s - jnp.max(s, axis=-1, keepdims=True))
        denom = jnp.sum(p, axis=-1, keepdims=True)
        w = p.astype(BF16)
        yield
        o = jnp.dot(w, vm_ref[:, sl], preferred_element_type=F32)
        obuf[:, sl] = (o * (1.0 / denom)).astype(BF16)
        yield

    def xout(c):
        sl = slice(c * Z_BLK, (c + 1) * Z_BLK)
        o_ref[:, sl] = h1_prev[:, sl] + jnp.dot(obuf[...], wxo_refs[c][...], preferred_element_type=F32)

    def cast_gu():
        wgu_out[...] = wgu_in[...].astype(BF16)

    def cast_d():
        wd_out[...] = wd_in[...].astype(BF16)

    part = functools.partial
    heads = [xhead(hd) for hd in range(X_HEADS)]
    depth = 3
    ticks = iter(range(len(heads) + depth - 1))

    def tick():
        t = next(ticks)
        for k in range(max(0, t - depth + 1), min(t + 1, len(heads))):
            next(heads[k])

    for thunk in (xnorm, part(merge_chunk, 0), xq, cast_gu, tick, part(merge_chunk, 1), tick, part(merge_chunk, 2),
                  tick, part(merge_chunk, 3), tick, part(out_chunk, 0), tick, part(out_chunk, 1), tick,
                  cast_d, part(xout, 0), part(out_chunk, 2), part(xout, 1), part(xout, 2),
                  part(out_chunk, 3), part(xout, 3)):
        thunk()
    assert next(ticks, None) is None


def _mix_out(a, cv, zg, x, w_attn_o, w_conv_o, w_out, g_xattn, w_xq, kt_mem, v_mem, w_xo, w_gate_up, w_down,
             tm):
    s, d = x.shape
    n = s // tm
    xw = w_xq.shape[1]
    assert d == N_GATE_BLKS * Z_BLK

    def cur(width):
        return pl.BlockSpec((tm, width), lambda i: (jnp.minimum(i, n - 1), 0))

    def col_block(w, k):
        return pl.BlockSpec((w.shape[0], Z_BLK), lambda i: (0, k), pipeline_mode=pl.Buffered(1))

    gu_spec = _cast_spec(w_gate_up, n)
    d_spec = _cast_spec(w_down, n)
    ws = (w_attn_o, w_conv_o, w_out)
    return pl.pallas_call(
        _mix_out_kernel,
        grid=(n + 1,),
        in_specs=[cur(a.shape[1]), cur(cv.shape[1]), cur(zg.shape[1]), cur(d)]
        + [col_block(w, k) for w in ws for k in range(N_GATE_BLKS)]
        + [_const_spec((1, d)), _const_spec((d, xw)), _const_spec(kt_mem.shape), _const_spec(v_mem.shape)]
        + [col_block(w_xo, k) for k in range(N_GATE_BLKS)]
        + [gu_spec, d_spec],
        out_specs=[pl.BlockSpec((tm, d), lambda i: (jnp.maximum(i - 1, 0), 0)), gu_spec, d_spec],
        out_shape=[
            jax.ShapeDtypeStruct((s, d), F32),
            jax.ShapeDtypeStruct(w_gate_up.shape, BF16),
            jax.ShapeDtypeStruct(w_down.shape, BF16),
        ],
        scratch_shapes=[
            pltpu.VMEM((tm, d), BF16),
            pltpu.VMEM((2, tm, d), F32),
            pltpu.VMEM((tm, xw), BF16),
        ],
        compiler_params=pltpu.CompilerParams(
            dimension_semantics=("arbitrary",), vmem_limit_bytes=VMEM_LIMIT),
        name="mix_out",
    )(a, cv, zg, x, *[w for w in ws for _ in range(N_GATE_BLKS)],
      g_xattn.reshape(1, d), w_xq, kt_mem, v_mem, *([w_xo] * N_GATE_BLKS), w_gate_up, w_down)


def _ffn_kernel(*refs, final_norm):
    nb = N_GATE_BLKS
    h_ref, g_ref, wg_ref, wu_ref = refs[:4]
    wd_refs = refs[4:4 + nb]
    gf_ref, o_ref, n_ref = refs[4 + nb:]
    j = pl.program_id(1)

    def chunk_step(first):
        if first:
            n = _rms(h_ref[...], g_ref[...]).astype(BF16)
            n_ref[...] = n
        else:
            n = n_ref[...]
        g = jnp.dot(n, wg_ref[...], preferred_element_type=F32)
        u = jnp.dot(n, wu_ref[...], preferred_element_type=F32)
        act = (g * _sigmoid(g) * u).astype(BF16)
        for c in range(nb):
            sl = slice(c * Z_BLK, (c + 1) * Z_BLK)
            base = h_ref[:, sl] if first else o_ref[:, sl]
            o_ref[:, sl] = base + jnp.dot(act, wd_refs[c][...], preferred_element_type=F32)

    pl.when(j == 0)(functools.partial(chunk_step, True))
    pl.when(j > 0)(functools.partial(chunk_step, False))

    if final_norm:
        @pl.when(j == pl.num_programs(1) - 1)
        def _():
            o_ref[...] = _rms(o_ref[...], gf_ref[...])


def _ffn(h, g, w_gate_up, w_down, g_final, tm, tf, final_norm):
    s, d = h.shape
    dff = w_down.shape[0]
    nf = dff // tf
    return pl.pallas_call(
        functools.partial(_ffn_kernel, final_norm=final_norm),
        grid=(s // tm, nf),
        in_specs=[
            pl.BlockSpec((tm, d), lambda i, j: (i, 0)),
            pl.BlockSpec((1, d), lambda i, j: (0, 0)),
            pl.BlockSpec((d, tf), lambda i, j: (0, j)),
            pl.BlockSpec((d, tf), lambda i, j: (0, j + nf)),
        ]
        + [pl.BlockSpec((tf, Z_BLK), functools.partial(lambda i, j, c: (j, c), c=c))
           for c in range(N_GATE_BLKS)]
        + [pl.BlockSpec((1, d), lambda i, j: (0, 0))],
        out_specs=pl.BlockSpec((tm, d), lambda i, j: (i, 0)),
        out_shape=jax.ShapeDtypeStruct((s, d), F32),
        scratch_shapes=[pltpu.VMEM((tm, d), BF16)],
        compiler_params=pltpu.CompilerParams(
            dimension_semantics=("arbitrary", "arbitrary"), vmem_limit_bytes=VMEM_LIMIT),
        name="ffn",
    )(h, g.reshape(1, d), w_gate_up, w_gate_up, *([w_down] * N_GATE_BLKS), g_final.reshape(1, d))


def _rope_table(s):
    inv_freq = ROPE_THETA ** (-np.arange(0, ROT_DIM, 2, dtype=np.float64) / ROT_DIM)
    ang = np.arange(s, dtype=np.float64)[:, None] * inv_freq[None, :]
    cos, sin = np.cos(ang), np.sin(ang)
    ones = np.ones((s, HEAD_DIM - ROT_DIM))
    zeros = np.zeros((s, HEAD_DIM - ROT_DIM))
    return jnp.asarray(np.concatenate([cos, cos, ones, -sin, sin, zeros], axis=1), dtype=F32)


def kernel(x, mem, g_mix, w_in, conv_w, sinks, w_attn_o, w_conv_o, w_out, g_xattn, g_mem,
           w_xq, w_xkv, w_xo, g_ffn, w_gate_up, w_down, g_final):
    batch, s, d = x.shape
    depth = w_in.shape[0]
    if depth < 1:
        raise ValueError("depth must be positive")
    tab = _rope_table(s)

    outs = []
    for b in range(batch):
        h = x[b]
        for l in range(depth):
            a, cv, zg, wao, wco, wout, wxq, wxo = _in_mix(
                h, g_mix[l], w_in[l].astype(BF16), tab, sinks[l], conv_w[l],
                (w_attn_o[l], w_conv_o[l], w_out[l], w_xq[l], w_xo[l]), tm=256)
            kt_mem, v_mem = _mem_kv(mem[b], g_mem[l], w_xkv[l])
            h, wgu, wdn = _mix_out(a, cv, zg, h, wao, wco, wout, g_xattn[l], wxq, kt_mem, v_mem, wxo,
                                   w_gate_up[l], w_down[l], tm=256)
            h = _ffn(h, g_ffn[l], wgu, wdn, g_final, tm=1024, tf=512, final_norm=(l == depth - 1))
        outs.append(h)
    return jnp.stack(outs, axis=0)
```

```python
import functools

import jax
import jax.numpy as jnp
import numpy as np
from jax import lax
from jax.experimental import pallas as pl
from jax.experimental.pallas import tpu as pltpu

F32 = jnp.float32
BF16 = jnp.bfloat16

EPS = 1e-6
HEAD_DIM = 64
N_Q_HEADS = 16
N_KV_HEADS = 4
Q_PER_KV = N_Q_HEADS // N_KV_HEADS
BLOCK = 128
ROT_DIM = HEAD_DIM // 4
ROPE_THETA = 500000.0
CONV_K = 3
X_HEADS = 4
X_HEAD_DIM = 128
LANES = 128
SUBLANES = 8

VMEM_LIMIT = 56 * 1024 * 1024


def _rms(x, g):
    ms = jnp.mean(x * x, axis=-1, keepdims=True)
    return x * lax.rsqrt(ms + EPS) * g


def _sigmoid(x):
    return 0.5 * jnp.tanh(0.5 * x) + 0.5


def _const_spec(shape):
    return pl.BlockSpec(shape, lambda i: (0,) * len(shape), pipeline_mode=pl.Buffered(1))


def _mem_kv_kernel(mem_ref, g_ref, w_ref, kt_ref, v_ref):
    n = _rms(mem_ref[...], g_ref[...]).astype(BF16)
    kv = jnp.dot(n, w_ref[...].astype(BF16), preferred_element_type=F32)
    xw = v_ref.shape[1]
    kt_ref[...] = kv[:, :xw].T.astype(BF16)
    v_ref[...] = kv[:, xw:].astype(BF16)


def _mem_kv(mem, g_mem, w_xkv):
    n_mem, d = mem.shape
    xw = w_xkv.shape[1] // 2
    return pl.pallas_call(
        _mem_kv_kernel,
        out_shape=[jax.ShapeDtypeStruct((xw, n_mem), BF16), jax.ShapeDtypeStruct((n_mem, xw), BF16)],
        compiler_params=pltpu.CompilerParams(vmem_limit_bytes=VMEM_LIMIT),
        name="mem_kv",
    )(mem, g_mem.reshape(1, d), w_xkv)


Z_BLK = 512
CH_Q, CH_KV, CH_CB, CH_CC, CH_CH, CH_GATE, CH_END = 0, 2, 3, 5, 7, 9, 17
N_GATE_BLKS = 4


def _in_mix_kernel(*refs, n_cast):
    x_ref, g_ref, w_ref, tab_ref, sink_ref, convw_ref = refs[:6]
    cast_in = refs[6:6 + n_cast]
    a_ref, cv_ref, zg_ref = refs[6 + n_cast:9 + n_cast]
    cast_out = refs[9 + n_cast:9 + 2 * n_cast]
    qbuf, kbuf, vbuf, vwin, cbuf, cbb = refs[9 + 2 * n_cast:]
    i = pl.program_id(0)
    tm = x_ref.shape[0]
    nblk = tm // BLOCK
    kvw = N_KV_HEADS * HEAD_DIM

    @pl.when(i == 0)
    def _():
        kbuf[0:BLOCK, :] = jnp.zeros((BLOCK, kvw), BF16)
        vbuf[:, 0:BLOCK] = jnp.zeros((kvw, BLOCK), BF16)
        cbuf[0:SUBLANES, :] = jnp.zeros((SUBLANES, cbuf.shape[1]), F32)

    @pl.when(i > 0)
    def _():
        kbuf[0:BLOCK, :] = kbuf[tm:tm + BLOCK, :]
        vbuf[:, 0:BLOCK] = vbuf[:, tm:tm + BLOCK]
        cbuf[0:SUBLANES, :] = cbuf[tm:tm + SUBLANES, :]

    n = _rms(x_ref[...], g_ref[...]).astype(BF16)

    def chunk(c):
        return jnp.dot(n, w_ref[:, c * Z_BLK:(c + 1) * Z_BLK], preferred_element_type=F32)

    half = ROT_DIM // 2
    tab = tab_ref[...]
    swapped = pltpu.roll(tab, HEAD_DIM, 1)
    lane = lax.broadcasted_iota(jnp.int32, tab.shape, 1)
    low = lane < HEAD_DIM
    hl = lane & (HEAD_DIM - 1)
    cos = jnp.where(low, tab, swapped)
    ss = jnp.where(low, swapped, tab)
    sa = jnp.where(hl < half, ss, 0.0)
    sb = jnp.where(hl >= half, ss, 0.0)

    def rope(t):
        return t * cos + pltpu.roll(t, LANES - half, 1) * sa + pltpu.roll(t, half, 1) * sb

    scale = HEAD_DIM ** -0.5

    def q_chunk(c):
        r = chunk(c)
        for g in range(Z_BLK // LANES):
            col = (c - CH_Q) * Z_BLK + g * LANES
            qbuf[col:col + LANES, :] = (rope(r[:, g * LANES:(g + 1) * LANES]) * scale).T.astype(BF16)

    def kv_chunk():
        r = chunk(CH_KV)
        for g in range(kvw // LANES):
            sl = slice(g * LANES, (g + 1) * LANES)
            kbuf[BLOCK:, sl] = rope(r[:, sl]).astype(BF16)
            vbuf[sl, BLOCK:] = r[:, kvw + g * LANES:kvw + (g + 1) * LANES].T.astype(BF16)
        lane = lax.broadcasted_iota(jnp.int32, (kvw, BLOCK), 1)
        for b in range(nblk):
            first = vbuf[:, b * BLOCK:(b + 1) * BLOCK].astype(F32)
            vwin[b, :, 0:BLOCK] = jnp.where(lane == 0, 0.0, first).astype(BF16)
            vwin[b, :, BLOCK:] = vbuf[:, (b + 1) * BLOCK:(b + 2) * BLOCK]

    def cb_chunk(c):
        cbb[:, (c - CH_CB) * Z_BLK:(c - CH_CB + 1) * Z_BLK] = chunk(c)

    def cc_chunk(c):
        cbuf[SUBLANES:, (c - CH_CC) * Z_BLK:(c - CH_CC + 1) * Z_BLK] = chunk(c)

    def ch_chunk(c):
        sl = slice((c - CH_CH) * Z_BLK, (c - CH_CH + 1) * Z_BLK)
        cbuf[SUBLANES:, sl] = cbuf[SUBLANES:, sl] * chunk(c)

    def gate_chunk(c):
        zg_ref[:, (c - CH_GATE) * Z_BLK:(c - CH_GATE + 1) * Z_BLK] = _sigmoid(chunk(c)).astype(BF16)

    cw = convw_ref[...]
    conv_cols = 2 * LANES

    def conv_chunk(c):
        sl = slice(c * conv_cols, (c + 1) * conv_cols)
        y = cw[2:3, sl] * cbuf[SUBLANES:, sl]
        y = y + cw[1:2, sl] * cbuf[SUBLANES - 1:SUBLANES - 1 + tm, sl]
        y = y + cw[0:1, sl] * cbuf[SUBLANES - 2:SUBLANES - 2 + tm, sl]
        cv_ref[:, sl] = (cbb[:, sl] * y).astype(BF16)

    pair_w = 2 * BLOCK
    ci = lax.broadcasted_iota(jnp.int32, (2 * BLOCK, pair_w), 0)
    qi = lax.broadcasted_iota(jnp.int32, (2 * BLOCK, pair_w), 1) & (BLOCK - 1)
    rel = qi + BLOCK - ci
    band = (rel >= 0) & (rel < BLOCK)
    band_first = band & (ci >= jnp.where(i == 0, BLOCK, 0))
    top_row = lax.broadcasted_iota(jnp.int32, (SUBLANES, pair_w), 0) == 0
    top_second = lax.broadcasted_iota(jnp.int32, (SUBLANES, pair_w), 1) >= BLOCK

    def attn_task(b, h0):
        g = h0 // Q_PER_KV
        rows = slice(b * BLOCK, (b + 1) * BLOCK)
        gcols = slice(g * HEAD_DIM, (g + 1) * HEAD_DIM)
        qcat = jnp.concatenate([qbuf[h * HEAD_DIM:(h + 1) * HEAD_DIM, rows] for h in (h0, h0 + 1)], axis=1)
        s = jnp.dot(kbuf[b * BLOCK:(b + 2) * BLOCK, gcols], qcat, preferred_element_type=F32)
        yield
        s = jnp.where(band_first if b == 0 else band, s, -jnp.inf)
        sink = jnp.where(top_second, sink_ref[h0 + 1], sink_ref[h0])
        s = jnp.concatenate([jnp.where(top_row, sink, s[0:SUBLANES]), s[SUBLANES:]], axis=0)
        m = jnp.max(s, axis=0, keepdims=True)
        p = jnp.exp(s - m)
        denom = jnp.sum(p, axis=0, keepdims=True)
        w = p.astype(BF16)
        yield
        o = jnp.dot(vwin[b, gcols, :], w, preferred_element_type=F32) * (1.0 / denom)
        both = jnp.concatenate([o[:, :BLOCK], o[:, BLOCK:]], axis=0)
        a_ref[rows, h0 * HEAD_DIM:(h0 + 2) * HEAD_DIM] = both.T.astype(BF16)
        yield

    tasks = [attn_task(b, h0) for b in range(nblk) for h0 in range(0, N_Q_HEADS, 2)]
    depth = 3
    n_ticks = len(tasks) + depth - 1
    ticks = iter(range(n_ticks))

    def run_ticks(count):
        for _ in range(count):
            t = next(ticks)
            for k in range(max(0, t - depth + 1), min(t + 1, len(tasks))):
                next(tasks[k])

    q_chunk(CH_Q)
    kv_chunk()
    rest = ([functools.partial(q_chunk, c) for c in range(CH_Q + 1, CH_KV)]
            + [functools.partial(cb_chunk, c) for c in range(CH_CB, CH_CC)]
            + [functools.partial(cc_chunk, c) for c in range(CH_CC, CH_CH)]
            + [functools.partial(ch_chunk, c) for c in range(CH_CH, CH_GATE)]
            + [functools.partial(gate_chunk, c) for c in range(CH_GATE, CH_END)])
    n_conv = cbuf.shape[1] // conv_cols
    span = len(rest) - 2
    for k, thunk in enumerate(rest):
        if k < span:
            run_ticks((k + 1) * n_ticks // span - k * n_ticks // span)
        thunk()
        kc = k - (CH_GATE - CH_CB) - (CH_KV - CH_Q - 1)
        if 0 <= kc < n_conv:
            conv_chunk(kc)
    assert next(ticks, None) is None
    for src, dst in zip(cast_in, cast_out):
        dst[...] = src[...].astype(BF16)


def _cast_spec(w, n_steps):
    rows = w.shape[0]
    pack = 2 * SUBLANES
    blk = pack
    while blk * n_steps < rows or rows % blk or n_steps % (rows // blk):
        blk += pack
    stride = n_steps // (rows // blk)
    return pl.BlockSpec((blk, w.shape[1]), lambda i: (jnp.minimum(i, n_steps - 1) // stride, 0))


def _in_mix(x, g, w, tab, sinks, conv_w, cast_ws, tm):
    s, d = x.shape
    cast_specs = [_cast_spec(cw_, s // tm) for cw_ in cast_ws]
    n = w.shape[1]
    aw = N_Q_HEADS * HEAD_DIM
    kvw = N_KV_HEADS * HEAD_DIM
    cw = conv_w.shape[1]
    gw = n - CH_GATE * Z_BLK
    assert aw == (CH_KV - CH_Q) * Z_BLK and 2 * kvw == Z_BLK and cw == (CH_CC - CH_CB) * Z_BLK
    assert n == CH_END * Z_BLK and gw == 2 * d
    return pl.pallas_call(
        functools.partial(_in_mix_kernel, n_cast=len(cast_ws)),
        grid=(s // tm,),
        in_specs=[
            pl.BlockSpec((tm, d), lambda i: (i, 0)),
            _const_spec((1, d)),
            _const_spec((d, n)),
            pl.BlockSpec((tm, LANES), lambda i: (i, 0)),
            pl.BlockSpec(memory_space=pltpu.SMEM),
            _const_spec((CONV_K, cw)),
        ] + cast_specs,
        out_specs=[
            pl.BlockSpec((tm, aw), lambda i: (i, 0)),
            pl.BlockSpec((tm, cw), lambda i: (i, 0)),
            pl.BlockSpec((tm, gw), lambda i: (i, 0)),
        ] + cast_specs,
        out_shape=[
            jax.ShapeDtypeStruct((s, aw), BF16),
            jax.ShapeDtypeStruct((s, cw), BF16),
            jax.ShapeDtypeStruct((s, gw), BF16),
        ] + [jax.ShapeDtypeStruct(cw_.shape, BF16) for cw_ in cast_ws],
        scratch_shapes=[
            pltpu.VMEM((aw, tm), BF16),
            pltpu.VMEM((tm + BLOCK, kvw), BF16),
            pltpu.VMEM((kvw, tm + BLOCK), BF16),
            pltpu.VMEM((tm // BLOCK, kvw, 2 * BLOCK), BF16),
            pltpu.VMEM((tm + SUBLANES, cw), F32),
            pltpu.VMEM((tm, cw), F32),
        ],
        compiler_params=pltpu.CompilerParams(
            dimension_semantics=("arbitrary",), vmem_limit_bytes=VMEM_LIMIT),
        name="in_mix",
    )(x, g.reshape(1, d), w, tab, sinks, conv_w, *cast_ws)


def _mix_out_kernel(*refs):
    nb = N_GATE_BLKS
    a_ref, cv_ref, zg_ref, x_ref = refs[:4]
    wao_refs, wco_refs, wout_refs = (refs[4 + k * nb:4 + (k + 1) * nb] for k in range(3))
    gx_ref, wxq_ref, kt_ref, vm_ref = refs[4 + 3 * nb:8 + 3 * nb]
    wxo_refs = refs[8 + 3 * nb:8 + 4 * nb]
    wgu_in, wd_in, o_ref, wgu_out, wd_out, mbuf, h1buf, obuf = refs[8 + 4 * nb:]
    d = x_ref.shape[1]
    i = pl.program_id(0)
    h1_cur = h1buf.at[i % 2]
    h1_prev = h1buf.at[(i + 1) % 2]

    @pl.when(i == 0)
    def _():
        h1buf[1] = jnp.zeros(h1buf.shape[1:], F32)

    def merge_chunk(c):
        sl = slice(c * Z_BLK, (c + 1) * Z_BLK)
        ya = jnp.dot(a_ref[...], wao_refs[c][...], preferred_element_type=F32)
        yc = jnp.dot(cv_ref[...], wco_refs[c][...], preferred_element_type=F32)
        ga = zg_ref[:, sl].astype(F32)
        gc = zg_ref[:, d + c * Z_BLK:d + (c + 1) * Z_BLK].astype(F32)
        mbuf[:, sl] = (ga * ya + gc * yc).astype(BF16)

    def out_chunk(c):
        sl = slice(c * Z_BLK, (c + 1) * Z_BLK)
        h1_cur[:, sl] = x_ref[:, sl] + jnp.dot(mbuf[...], wout_refs[c][...], preferred_element_type=F32)

    state = {}

    def xnorm():
        state["n"] = _rms(h1_prev[...], gx_ref[...]).astype(BF16)

    def xq():
        q = jnp.dot(state.pop("n"), wxq_ref[...], preferred_element_type=F32) * (X_HEAD_DIM ** -0.5)
        state["q"] = q.astype(BF16)

    def xhead(hd):
        sl = slice(hd * X_HEAD_DIM, (hd + 1) * X_HEAD_DIM)
        s = jnp.dot(state["q"][:, sl], kt_ref[sl, :], preferred_element_type=F32)
        yield
        p = jnp.exp(s - jnp.max(s, axis=-1, keepdims=True))
        denom = jnp.sum(p, axis=-1, keepdims=True)
        w = p.astype(BF16)
        yield
        o = jnp.dot(w, vm_ref[:, sl], preferred_element_type=F32)
        obuf[:, sl] = (o * (1.0 / denom)).astype(BF16)
        yield

    def xout(c):
        sl = slice(c * Z_BLK, (c + 1) * Z_BLK)
        o_ref[:, sl] = h1_prev[:, sl] + jnp.dot(obuf[...], wxo_refs[c][...], preferred_element_type=F32)

    def cast_gu():
        wgu_out[...] = wgu_in[...].astype(BF16)

    def cast_d():
        wd_out[...] = wd_in[...].astype(BF16)

    part = functools.partial
    heads = [xhead(hd) for hd in range(X_HEADS)]
    depth = 3
    ticks = iter(range(len(heads) + depth - 1))

    def tick():
        t = next(ticks)
        for k in range(max(0, t - depth + 1), min(t + 1, len(heads))):
            next(heads[k])

    for thunk in (xnorm, part(merge_chunk, 0), xq, cast_gu, tick, part(merge_chunk, 1), tick, part(merge_chunk, 2),
                  tick, part(merge_chunk, 3), tick, part(out_chunk, 0), tick, part(out_chunk, 1), tick,
                  cast_d, part(xout, 0), part(out_chunk, 2), part(xout, 1), part(xout, 2),
                  part(out_chunk, 3), part(xout, 3)):
        thunk()
    assert next(ticks, None) is None


def _mix_out(a, cv, zg, x, w_attn_o, w_conv_o, w_out, g_xattn, w_xq, kt_mem, v_mem, w_xo, w_gate_up, w_down,
             tm):
    s, d = x.shape
    n = s // tm
    xw = w_xq.shape[1]
    assert d == N_GATE_BLKS * Z_BLK

    def cur(width):
        return pl.BlockSpec((tm, width), lambda i: (jnp.minimum(i, n - 1), 0))

    def col_block(w, k):
        return pl.BlockSpec((w.shape[0], Z_BLK), lambda i: (0, k), pipeline_mode=pl.Buffered(1))

    gu_spec = _cast_spec(w_gate_up, n)
    d_spec = _cast_spec(w_down, n)
    ws = (w_attn_o, w_conv_o, w_out)
    return pl.pallas_call(
        _mix_out_kernel,
        grid=(n + 1,),
        in_specs=[cur(a.shape[1]), cur(cv.shape[1]), cur(zg.shape[1]), cur(d)]
        + [col_block(w, k) for w in ws for k in range(N_GATE_BLKS)]
        + [_const_spec((1, d)), _const_spec((d, xw)), _const_spec(kt_mem.shape), _const_spec(v_mem.shape)]
        + [col_block(w_xo, k) for k in range(N_GATE_BLKS)]
        + [gu_spec, d_spec],
        out_specs=[pl.BlockSpec((tm, d), lambda i: (jnp.maximum(i - 1, 0), 0)), gu_spec, d_spec],
        out_shape=[
            jax.ShapeDtypeStruct((s, d), F32),
            jax.ShapeDtypeStruct(w_gate_up.shape, BF16),
            jax.ShapeDtypeStruct(w_down.shape, BF16),
        ],
        scratch_shapes=[
            pltpu.VMEM((tm, d), BF16),
            pltpu.VMEM((2, tm, d), F32),
            pltpu.VMEM((tm, xw), BF16),
        ],
        compiler_params=pltpu.CompilerParams(
            dimension_semantics=("arbitrary",), vmem_limit_bytes=VMEM_LIMIT),
        name="mix_out",
    )(a, cv, zg, x, *[w for w in ws for _ in range(N_GATE_BLKS)],
      g_xattn.reshape(1, d), w_xq, kt_mem, v_mem, *([w_xo] * N_GATE_BLKS), w_gate_up, w_down)


def _ffn_kernel(*refs, final_norm, nf):
    nb = N_GATE_BLKS
    h_ref, g_ref, wg_ref, wu_ref = refs[:4]
    wd_refs = refs[4:4 + nb]
    gf_ref, o_ref, n_ref = refs[4 + nb:]
    j = pl.program_id(1)

    def chunk_step(first, last):
        if first:
            n = _rms(h_ref[...], g_ref[...]).astype(BF16)
            n_ref[...] = n
        else:
            n = n_ref[...]
        g = jnp.dot(n, wg_ref[...], preferred_element_type=F32)
        u = jnp.dot(n, wu_ref[...], preferred_element_type=F32)
        act = (g * _sigmoid(g) * u).astype(BF16)
        ssq = None
        for c in range(nb):
            sl = slice(c * Z_BLK, (c + 1) * Z_BLK)
            base = h_ref[:, sl] if first else o_ref[:, sl]
            oc = base + jnp.dot(act, wd_refs[c][...], preferred_element_type=F32)
            o_ref[:, sl] = oc
            if last:
                sq = jnp.sum(oc * oc, axis=-1, keepdims=True)
                ssq = sq if ssq is None else ssq + sq
        if last:
            inv = lax.rsqrt(ssq / o_ref.shape[1] + EPS)
            for c in range(nb):
                sl = slice(c * Z_BLK, (c + 1) * Z_BLK)
                o_ref[:, sl] = o_ref[:, sl] * inv * gf_ref[:, sl]

    if nf == 1:
        chunk_step(True, final_norm)
    else:
        pl.when(j == 0)(functools.partial(chunk_step, True, False))
        pl.when(jnp.logical_and(j > 0, j < nf - 1))(functools.partial(chunk_step, False, False))
        pl.when(j == nf - 1)(functools.partial(chunk_step, False, final_norm))


def _ffn(h, g, w_gate_up, w_down, g_final, tm, tf, final_norm):
    s, d = h.shape
    dff = w_down.shape[0]
    nf = dff // tf
    return pl.pallas_call(
        functools.partial(_ffn_kernel, final_norm=final_norm, nf=nf),
        grid=(s // tm, nf),
        in_specs=[
            pl.BlockSpec((tm, d), lambda i, j: (i, 0)),
            pl.BlockSpec((1, d), lambda i, j: (0, 0)),
            pl.BlockSpec((d, tf), lambda i, j: (0, j)),
            pl.BlockSpec((d, tf), lambda i, j: (0, j + nf)),
        ]
        + [pl.BlockSpec((tf, Z_BLK), functools.partial(lambda i, j, c: (j, c), c=c))
           for c in range(N_GATE_BLKS)]
        + [pl.BlockSpec((1, d), lambda i, j: (0, 0))],
        out_specs=pl.BlockSpec((tm, d), lambda i, j: (i, 0)),
        out_shape=jax.ShapeDtypeStruct((s, d), F32),
        scratch_shapes=[pltpu.VMEM((tm, d), BF16)],
        compiler_params=pltpu.CompilerParams(
            dimension_semantics=("arbitrary", "arbitrary"), vmem_limit_bytes=VMEM_LIMIT),
        name="ffn",
    )(h, g.reshape(1, d), w_gate_up, w_gate_up, *([w_down] * N_GATE_BLKS), g_final.reshape(1, d))


def _rope_table(s):
    inv_freq = ROPE_THETA ** (-np.arange(0, ROT_DIM, 2, dtype=np.float64) / ROT_DIM)
    ang = np.arange(s, dtype=np.float64)[:, None] * inv_freq[None, :]
    cos, sin = np.cos(ang), np.sin(ang)
    ones = np.ones((s, HEAD_DIM - ROT_DIM))
    zeros = np.zeros((s, HEAD_DIM - ROT_DIM))
    return jnp.asarray(np.concatenate([cos, cos, ones, -sin, sin, zeros], axis=1), dtype=F32)


def kernel(x, mem, g_mix, w_in, conv_w, sinks, w_attn_o, w_conv_o, w_out, g_xattn, g_mem,
           w_xq, w_xkv, w_xo, g_ffn, w_gate_up, w_down, g_final):
    batch, s, d = x.shape
    depth = w_in.shape[0]
    if depth < 1:
        raise ValueError("depth must be positive")
    tab = _rope_table(s)

    outs = []
    for b in range(batch):
        h = x[b]
        for l in range(depth):
            a, cv, zg, wao, wco, wout, wxq, wxo = _in_mix(
                h, g_mix[l], w_in[l].astype(BF16), tab, sinks[l], conv_w[l],
                (w_attn_o[l], w_conv_o[l], w_out[l], w_xq[l], w_xo[l]), tm=256)
            kt_mem, v_mem = _mem_kv(mem[b], g_mem[l], w_xkv[l])
            h, wgu, wdn = _mix_out(a, cv, zg, h, wao, wco, wout, g_xattn[l], wxq, kt_mem, v_mem, wxo,
                                   w_gate_up[l], w_down[l], tm=256)
            h = _ffn(h, g_ffn[l], wgu, wdn, g_final, tm=1024, tf=512, final_norm=(l == depth - 1))
        outs.append(h)
    return jnp.stack(outs, axis=0)
```

```python
import functools

import jax
import jax.numpy as jnp
import numpy as np
from jax import lax
from jax.experimental import pallas as pl
from jax.experimental.pallas import tpu as pltpu

F32 = jnp.float32
BF16 = jnp.bfloat16

EPS = 1e-6
HEAD_DIM = 64
N_Q_HEADS = 16
N_KV_HEADS = 4
Q_PER_KV = N_Q_HEADS // N_KV_HEADS
BLOCK = 128
ROT_DIM = HEAD_DIM // 4
ROPE_THETA = 500000.0
CONV_K = 3
X_HEADS = 4
X_HEAD_DIM = 128
LANES = 128
SUBLANES = 8

VMEM_LIMIT = 56 * 1024 * 1024


def _rms(x, g):
    ms = jnp.mean(x * x, axis=-1, keepdims=True)
    return x * lax.rsqrt(ms + EPS) * g


def _sigmoid(x):
    return 0.5 * jnp.tanh(0.5 * x) + 0.5


def _const_spec(shape):
    return pl.BlockSpec(shape, lambda i: (0,) * len(shape), pipeline_mode=pl.Buffered(1))


def _mem_kv_kernel(mem_ref, g_ref, w_ref, kt_ref, v_ref):
    n = _rms(mem_ref[...], g_ref[...]).astype(BF16)
    kv = jnp.dot(n, w_ref[...].astype(BF16), preferred_element_type=F32)
    xw = v_ref.shape[1]
    kt_ref[...] = kv[:, :xw].T.astype(BF16)
    v_ref[...] = kv[:, xw:].astype(BF16)


def _mem_kv(mem, g_mem, w_xkv):
    n_mem, d = mem.shape
    xw = w_xkv.shape[1] // 2
    return pl.pallas_call(
        _mem_kv_kernel,
        out_shape=[jax.ShapeDtypeStruct((xw, n_mem), BF16), jax.ShapeDtypeStruct((n_mem, xw), BF16)],
        compiler_params=pltpu.CompilerParams(vmem_limit_bytes=VMEM_LIMIT),
        name="mem_kv",
    )(mem, g_mem.reshape(1, d), w_xkv)


Z_BLK = 512
CH_Q, CH_KV, CH_CB, CH_CC, CH_CH, CH_GATE, CH_END = 0, 2, 3, 5, 7, 9, 17
N_GATE_BLKS = 4


def _in_mix_kernel(*refs, n_cast):
    x_ref, g_ref, w_ref, tab_ref, sink_ref, convw_ref = refs[:6]
    cast_in = refs[6:6 + n_cast]
    a_ref, cv_ref, zg_ref = refs[6 + n_cast:9 + n_cast]
    cast_out = refs[9 + n_cast:9 + 2 * n_cast]
    qbuf, kbuf, vbuf, vwin, cbuf, cbb = refs[9 + 2 * n_cast:]
    i = pl.program_id(0)
    tm = x_ref.shape[0]
    nblk = tm // BLOCK
    kvw = N_KV_HEADS * HEAD_DIM

    @pl.when(i == 0)
    def _():
        kbuf[0:BLOCK, :] = jnp.zeros((BLOCK, kvw), BF16)
        vbuf[:, 0:BLOCK] = jnp.zeros((kvw, BLOCK), BF16)
        cbuf[0:SUBLANES, :] = jnp.zeros((SUBLANES, cbuf.shape[1]), F32)

    @pl.when(i > 0)
    def _():
        kbuf[0:BLOCK, :] = kbuf[tm:tm + BLOCK, :]
        vbuf[:, 0:BLOCK] = vbuf[:, tm:tm + BLOCK]
        cbuf[0:SUBLANES, :] = cbuf[tm:tm + SUBLANES, :]

    n = _rms(x_ref[...], g_ref[...]).astype(BF16)

    def chunk(c):
        return jnp.dot(n, w_ref[:, c * Z_BLK:(c + 1) * Z_BLK], preferred_element_type=F32)

    half = ROT_DIM // 2
    tab = tab_ref[...]
    swapped = pltpu.roll(tab, HEAD_DIM, 1)
    lane = lax.broadcasted_iota(jnp.int32, tab.shape, 1)
    low = lane < HEAD_DIM
    hl = lane & (HEAD_DIM - 1)
    cos = jnp.where(low, tab, swapped)
    ss = jnp.where(low, swapped, tab)
    sa = jnp.where(hl < half, ss, 0.0)
    sb = jnp.where(hl >= half, ss, 0.0)

    def rope(t):
        return t * cos + pltpu.roll(t, LANES - half, 1) * sa + pltpu.roll(t, half, 1) * sb

    scale = HEAD_DIM ** -0.5

    def q_chunk(c):
        r = chunk(c)
        for g in range(Z_BLK // LANES):
            col = (c - CH_Q) * Z_BLK + g * LANES
            qbuf[col:col + LANES, :] = (rope(r[:, g * LANES:(g + 1) * LANES]) * scale).T.astype(BF16)

    def kv_chunk():
        r = chunk(CH_KV)
        for g in range(kvw // LANES):
            sl = slice(g * LANES, (g + 1) * LANES)
            kbuf[BLOCK:, sl] = rope(r[:, sl]).astype(BF16)
            vbuf[sl, BLOCK:] = r[:, kvw + g * LANES:kvw + (g + 1) * LANES].T.astype(BF16)
        lane = lax.broadcasted_iota(jnp.int32, (kvw, BLOCK), 1)
        for b in range(nblk):
            first = vbuf[:, b * BLOCK:(b + 1) * BLOCK].astype(F32)
            vwin[b, :, 0:BLOCK] = jnp.where(lane == 0, 0.0, first).astype(BF16)
            vwin[b, :, BLOCK:] = vbuf[:, (b + 1) * BLOCK:(b + 2) * BLOCK]

    def cb_chunk(c):
        cbb[:, (c - CH_CB) * Z_BLK:(c - CH_CB + 1) * Z_BLK] = chunk(c)

    def cc_chunk(c):
        cbuf[SUBLANES:, (c - CH_CC) * Z_BLK:(c - CH_CC + 1) * Z_BLK] = chunk(c)

    def ch_chunk(c):
        sl = slice((c - CH_CH) * Z_BLK, (c - CH_CH + 1) * Z_BLK)
        cbuf[SUBLANES:, sl] = cbuf[SUBLANES:, sl] * chunk(c)

    def gate_chunk(c):
        zg_ref[:, (c - CH_GATE) * Z_BLK:(c - CH_GATE + 1) * Z_BLK] = _sigmoid(chunk(c)).astype(BF16)

    cw = convw_ref[...]
    conv_cols = 2 * LANES

    def conv_chunk(c):
        sl = slice(c * conv_cols, (c + 1) * conv_cols)
        y = cw[2:3, sl] * cbuf[SUBLANES:, sl]
        y = y + cw[1:2, sl] * cbuf[SUBLANES - 1:SUBLANES - 1 + tm, sl]
        y = y + cw[0:1, sl] * cbuf[SUBLANES - 2:SUBLANES - 2 + tm, sl]
        cv_ref[:, sl] = (cbb[:, sl] * y).astype(BF16)

    pair_w = 2 * BLOCK
    ci = lax.broadcasted_iota(jnp.int32, (2 * BLOCK, pair_w), 0)
    qi = lax.broadcasted_iota(jnp.int32, (2 * BLOCK, pair_w), 1) & (BLOCK - 1)
    rel = qi + BLOCK - ci
    band = (rel >= 0) & (rel < BLOCK)
    band_first = band & (ci >= jnp.where(i == 0, BLOCK, 0))
    top_row = lax.broadcasted_iota(jnp.int32, (SUBLANES, pair_w), 0) == 0
    top_second = lax.broadcasted_iota(jnp.int32, (SUBLANES, pair_w), 1) >= BLOCK

    def attn_task(b, h0):
        g = h0 // Q_PER_KV
        rows = slice(b * BLOCK, (b + 1) * BLOCK)
        gcols = slice(g * HEAD_DIM, (g + 1) * HEAD_DIM)
        qcat = jnp.concatenate([qbuf[h * HEAD_DIM:(h + 1) * HEAD_DIM, rows] for h in (h0, h0 + 1)], axis=1)
        s = jnp.dot(kbuf[b * BLOCK:(b + 2) * BLOCK, gcols], qcat, preferred_element_type=F32)
        yield
        s = jnp.where(band_first if b == 0 else band, s, -jnp.inf)
        sink = jnp.where(top_second, sink_ref[h0 + 1], sink_ref[h0])
        s = jnp.concatenate([jnp.where(top_row, sink, s[0:SUBLANES]), s[SUBLANES:]], axis=0)
        m = jnp.max(s, axis=0, keepdims=True)
        p = jnp.exp(s - m)
        denom = jnp.sum(p, axis=0, keepdims=True)
        w = p.astype(BF16)
        yield
        o = jnp.dot(vwin[b, gcols, :], w, preferred_element_type=F32) * (1.0 / denom)
        both = jnp.concatenate([o[:, :BLOCK], o[:, BLOCK:]], axis=0)
        a_ref[rows, h0 * HEAD_DIM:(h0 + 2) * HEAD_DIM] = both.T.astype(BF16)
        yield

    tasks = [attn_task(b, h0) for b in range(nblk) for h0 in range(0, N_Q_HEADS, 2)]
    depth = 3
    n_ticks = len(tasks) + depth - 1
    ticks = iter(range(n_ticks))

    def run_ticks(count):
        for _ in range(count):
            t = next(ticks)
            for k in range(max(0, t - depth + 1), min(t + 1, len(tasks))):
                next(tasks[k])

    q_chunk(CH_Q)
    kv_chunk()
    rest = ([functools.partial(q_chunk, c) for c in range(CH_Q + 1, CH_KV)]
            + [functools.partial(cb_chunk, c) for c in range(CH_CB, CH_CC)]
            + [functools.partial(cc_chunk, c) for c in range(CH_CC, CH_CH)]
            + [functools.partial(ch_chunk, c) for c in range(CH_CH, CH_GATE)]
            + [functools.partial(gate_chunk, c) for c in range(CH_GATE, CH_END)])
    n_conv = cbuf.shape[1] // conv_cols
    span = len(rest) - 2
    for k, thunk in enumerate(rest):
        if k < span:
            run_ticks((k + 1) * n_ticks // span - k * n_ticks // span)
        thunk()
        kc = k - (CH_GATE - CH_CB) - (CH_KV - CH_Q - 1)
        if 0 <= kc < n_conv:
            conv_chunk(kc)
    assert next(ticks, None) is None
    for src, dst in zip(cast_in, cast_out):
        dst[...] = src[...].astype(BF16)


def _cast_spec(w, n_steps):
    rows = w.shape[0]
    pack = 2 * SUBLANES
    blk = pack
    while blk * n_steps < rows or rows % blk or n_steps % (rows // blk):
        blk += pack
    stride = n_steps // (rows // blk)
    return pl.BlockSpec((blk, w.shape[1]), lambda i: (jnp.minimum(i, n_steps - 1) // stride, 0))


def _in_mix(x, g, w, tab, sinks, conv_w, cast_ws, tm):
    s, d = x.shape
    cast_specs = [_cast_spec(cw_, s // tm) for cw_ in cast_ws]
    n = w.shape[1]
    aw = N_Q_HEADS * HEAD_DIM
    kvw = N_KV_HEADS * HEAD_DIM
    cw = conv_w.shape[1]
    gw = n - CH_GATE * Z_BLK
    assert aw == (CH_KV - CH_Q) * Z_BLK and 2 * kvw == Z_BLK and cw == (CH_CC - CH_CB) * Z_BLK
    assert n == CH_END * Z_BLK and gw == 2 * d
    return pl.pallas_call(
        functools.partial(_in_mix_kernel, n_cast=len(cast_ws)),
        grid=(s // tm,),
        in_specs=[
            pl.BlockSpec((tm, d), lambda i: (i, 0)),
            _const_spec((1, d)),
            _const_spec((d, n)),
            pl.BlockSpec((tm, LANES), lambda i: (i, 0)),
            pl.BlockSpec(memory_space=pltpu.SMEM),
            _const_spec((CONV_K, cw)),
        ] + cast_specs,
        out_specs=[
            pl.BlockSpec((tm, aw), lambda i: (i, 0)),
            pl.BlockSpec((tm, cw), lambda i: (i, 0)),
            pl.BlockSpec((tm, gw), lambda i: (i, 0)),
        ] + cast_specs,
        out_shape=[
            jax.ShapeDtypeStruct((s, aw), BF16),
            jax.ShapeDtypeStruct((s, cw), BF16),
            jax.ShapeDtypeStruct((s, gw), BF16),
        ] + [jax.ShapeDtypeStruct(cw_.shape, BF16) for cw_ in cast_ws],
        scratch_shapes=[
            pltpu.VMEM((aw, tm), BF16),
            pltpu.VMEM((tm + BLOCK, kvw), BF16),
            pltpu.VMEM((kvw, tm + BLOCK), BF16),
            pltpu.VMEM((tm // BLOCK, kvw, 2 * BLOCK), BF16),
            pltpu.VMEM((tm + SUBLANES, cw), F32),
            pltpu.VMEM((tm, cw), F32),
        ],
        compiler_params=pltpu.CompilerParams(
            dimension_semantics=("arbitrary",), vmem_limit_bytes=VMEM_LIMIT),
        name="in_mix",
    )(x, g.reshape(1, d), w, tab, sinks, conv_w, *cast_ws)


def _mix_out_kernel(*refs, n_tiles):
    nb = N_GATE_BLKS
    a_ref, cv_ref, zg_ref, x_ref = refs[:4]
    wao_refs, wco_refs, wout_refs = (refs[4 + k * nb:4 + (k + 1) * nb] for k in range(3))
    gx_ref, wxq_ref, kt_ref, vm_ref = refs[4 + 3 * nb:8 + 3 * nb]
    wxo_refs = refs[8 + 3 * nb:8 + 4 * nb]
    wgu_in, wd_in, o_ref, wgu_out, wd_out, mbuf, h1buf, obuf = refs[8 + 4 * nb:]
    d = x_ref.shape[1]
    i = pl.program_id(0)
    h1_cur = h1buf.at[i % 2]
    h1_prev = h1buf.at[(i + 1) % 2]

    def merge_chunk(c):
        sl = slice(c * Z_BLK, (c + 1) * Z_BLK)
        ya = jnp.dot(a_ref[...], wao_refs[c][...], preferred_element_type=F32)
        yc = jnp.dot(cv_ref[...], wco_refs[c][...], preferred_element_type=F32)
        ga = zg_ref[:, sl].astype(F32)
        gc = zg_ref[:, d + c * Z_BLK:d + (c + 1) * Z_BLK].astype(F32)
        mbuf[:, sl] = (ga * ya + gc * yc).astype(BF16)

    def out_chunk(c):
        sl = slice(c * Z_BLK, (c + 1) * Z_BLK)
        h1_cur[:, sl] = x_ref[:, sl] + jnp.dot(mbuf[...], wout_refs[c][...], preferred_element_type=F32)

    state = {}

    def xnorm():
        state["n"] = _rms(h1_prev[...], gx_ref[...]).astype(BF16)

    def xq():
        q = jnp.dot(state.pop("n"), wxq_ref[...], preferred_element_type=F32) * (X_HEAD_DIM ** -0.5)
        state["q"] = q.astype(BF16)

    def xhead(hd):
        sl = slice(hd * X_HEAD_DIM, (hd + 1) * X_HEAD_DIM)
        s = jnp.dot(state["q"][:, sl], kt_ref[sl, :], preferred_element_type=F32)
        yield
        p = jnp.exp(s - jnp.max(s, axis=-1, keepdims=True))
        denom = jnp.sum(p, axis=-1, keepdims=True)
        w = p.astype(BF16)
        yield
        o = jnp.dot(w, vm_ref[:, sl], preferred_element_type=F32)
        obuf[:, sl] = (o * (1.0 / denom)).astype(BF16)
        yield

    def xout(c):
        sl = slice(c * Z_BLK, (c + 1) * Z_BLK)
        o_ref[:, sl] = h1_prev[:, sl] + jnp.dot(obuf[...], wxo_refs[c][...], preferred_element_type=F32)

    def cast_gu():
        wgu_out[...] = wgu_in[...].astype(BF16)

    def cast_d():
        wd_out[...] = wd_in[...].astype(BF16)

    part = functools.partial
    depth = 3

    def emit(proj, xattn):
        heads = [xhead(hd) for hd in range(X_HEADS)]
        ticks = iter(range(len(heads) + depth - 1))

        def tick():
            t = next(ticks)
            for k in range(max(0, t - depth + 1), min(t + 1, len(heads))):
                next(heads[k])

        p, x, c = "proj", "xattn", "cast"
        for kind, thunk in ((x, xnorm), (p, part(merge_chunk, 0)), (x, xq), (c, cast_gu), (x, tick),
                            (p, part(merge_chunk, 1)), (x, tick), (p, part(merge_chunk, 2)), (x, tick),
                            (p, part(merge_chunk, 3)), (x, tick), (p, part(out_chunk, 0)), (x, tick),
                            (p, part(out_chunk, 1)), (x, tick), (c, cast_d), (x, part(xout, 0)),
                            (p, part(out_chunk, 2)), (x, part(xout, 1)), (x, part(xout, 2)),
                            (p, part(out_chunk, 3)), (x, part(xout, 3))):
            if kind == c or (kind == p and proj) or (kind == x and xattn):
                thunk()

    pl.when(i == 0)(part(emit, True, False))
    pl.when(jnp.logical_and(i > 0, i < n_tiles))(part(emit, True, True))
    pl.when(i == n_tiles)(part(emit, False, True))


def _mix_out(a, cv, zg, x, w_attn_o, w_conv_o, w_out, g_xattn, w_xq, kt_mem, v_mem, w_xo, w_gate_up, w_down,
             tm):
    s, d = x.shape
    n = s // tm
    xw = w_xq.shape[1]
    assert d == N_GATE_BLKS * Z_BLK

    def cur(width):
        return pl.BlockSpec((tm, width), lambda i: (jnp.minimum(i, n - 1), 0))

    def col_block(w, k):
        return pl.BlockSpec((w.shape[0], Z_BLK), lambda i: (0, k), pipeline_mode=pl.Buffered(1))

    gu_spec = _cast_spec(w_gate_up, n)
    d_spec = _cast_spec(w_down, n)
    ws = (w_attn_o, w_conv_o, w_out)
    return pl.pallas_call(
        functools.partial(_mix_out_kernel, n_tiles=n),
        grid=(n + 1,),
        in_specs=[cur(a.shape[1]), cur(cv.shape[1]), cur(zg.shape[1]), cur(d)]
        + [col_block(w, k) for w in ws for k in range(N_GATE_BLKS)]
        + [_const_spec((1, d)), _const_spec((d, xw)), _const_spec(kt_mem.shape), _const_spec(v_mem.shape)]
        + [col_block(w_xo, k) for k in range(N_GATE_BLKS)]
        + [gu_spec, d_spec],
        out_specs=[pl.BlockSpec((tm, d), lambda i: (jnp.maximum(i - 1, 0), 0)), gu_spec, d_spec],
        out_shape=[
            jax.ShapeDtypeStruct((s, d), F32),
            jax.ShapeDtypeStruct(w_gate_up.shape, BF16),
            jax.ShapeDtypeStruct(w_down.shape, BF16),
        ],
        scratch_shapes=[
            pltpu.VMEM((tm, d), BF16),
            pltpu.VMEM((2, tm, d), F32),
            pltpu.VMEM((tm, xw), BF16),
        ],
        compiler_params=pltpu.CompilerParams(
            dimension_semantics=("arbitrary",), vmem_limit_bytes=VMEM_LIMIT),
        name="mix_out",
    )(a, cv, zg, x, *[w for w in ws for _ in range(N_GATE_BLKS)],
      g_xattn.reshape(1, d), w_xq, kt_mem, v_mem, *([w_xo] * N_GATE_BLKS), w_gate_up, w_down)


def _ffn_kernel(*refs, final_norm):
    nb = N_GATE_BLKS
    h_ref, g_ref, wg_ref, wu_ref = refs[:4]
    wd_refs = refs[4:4 + nb]
    gf_ref, o_ref, n_ref = refs[4 + nb:]
    j = pl.program_id(1)

    def chunk_step(first):
        if first:
            n = _rms(h_ref[...], g_ref[...]).astype(BF16)
            n_ref[...] = n
        else:
            n = n_ref[...]
        g = jnp.dot(n, wg_ref[...], preferred_element_type=F32)
        u = jnp.dot(n, wu_ref[...], preferred_element_type=F32)
        act = (g * _sigmoid(g) * u).astype(BF16)
        for c in range(nb):
            sl = slice(c * Z_BLK, (c + 1) * Z_BLK)
            base = h_ref[:, sl] if first else o_ref[:, sl]
            o_ref[:, sl] = base + jnp.dot(act, wd_refs[c][...], preferred_element_type=F32)

    pl.when(j == 0)(functools.partial(chunk_step, True))
    pl.when(j > 0)(functools.partial(chunk_step, False))

    if final_norm:
        @pl.when(j == pl.num_programs(1) - 1)
        def _():
            o_ref[...] = _rms(o_ref[...], gf_ref[...])


def _ffn(h, g, w_gate_up, w_down, g_final, tm, tf, final_norm):
    s, d = h.shape
    dff = w_down.shape[0]
    nf = dff // tf
    return pl.pallas_call(
        functools.partial(_ffn_kernel, final_norm=final_norm),
        grid=(s // tm, nf),
        in_specs=[
            pl.BlockSpec((tm, d), lambda i, j: (i, 0)),
            pl.BlockSpec((1, d), lambda i, j: (0, 0)),
            pl.BlockSpec((d, tf), lambda i, j: (0, j)),
            pl.BlockSpec((d, tf), lambda i, j: (0, j + nf)),
        ]
        + [pl.BlockSpec((tf, Z_BLK), functools.partial(lambda i, j, c: (j, c), c=c))
           for c in range(N_GATE_BLKS)]
        + [pl.BlockSpec((1, d), lambda i, j: (0, 0))],
        out_specs=pl.BlockSpec((tm, d), lambda i, j: (i, 0)),
        out_shape=jax.ShapeDtypeStruct((s, d), F32),
        scratch_shapes=[pltpu.VMEM((tm, d), BF16)],
        compiler_params=pltpu.CompilerParams(
            dimension_semantics=("arbitrary", "arbitrary"), vmem_limit_bytes=VMEM_LIMIT),
        name="ffn",
    )(h, g.reshape(1, d), w_gate_up, w_gate_up, *([w_down] * N_GATE_BLKS), g_final.reshape(1, d))


def _rope_table(s):
    inv_freq = ROPE_THETA ** (-np.arange(0, ROT_DIM, 2, dtype=np.float64) / ROT_DIM)
    ang = np.arange(s, dtype=np.float64)[:, None] * inv_freq[None, :]
    cos, sin = np.cos(ang), np.sin(ang)
    ones = np.ones((s, HEAD_DIM - ROT_DIM))
    zeros = np.zeros((s, HEAD_DIM - ROT_DIM))
    return jnp.asarray(np.concatenate([cos, cos, ones, -sin, sin, zeros], axis=1), dtype=F32)


def kernel(x, mem, g_mix, w_in, conv_w, sinks, w_attn_o, w_conv_o, w_out, g_xattn, g_mem,
           w_xq, w_xkv, w_xo, g_ffn, w_gate_up, w_down, g_final):
    batch, s, d = x.shape
    depth = w_in.shape[0]
    if depth < 1:
        raise ValueError("depth must be positive")
    tab = _rope_table(s)

    outs = []
    for b in range(batch):
        h = x[b]
        for l in range(depth):
            a, cv, zg, wao, wco, wout, wxq, wxo = _in_mix(
                h, g_mix[l], w_in[l].astype(BF16), tab, sinks[l], conv_w[l],
                (w_attn_o[l], w_conv_o[l], w_out[l], w_xq[l], w_xo[l]), tm=256)
            kt_mem, v_mem = _mem_kv(mem[b], g_mem[l], w_xkv[l])
            h, wgu, wdn = _mix_out(a, cv, zg, h, wao, wco, wout, g_xattn[l], wxq, kt_mem, v_mem, wxo,
                                   w_gate_up[l], w_down[l], tm=256)
            h = _ffn(h, g_ffn[l], wgu, wdn, g_final, tm=1024, tf=512, final_norm=(l == depth - 1))
        outs.append(h)
    return jnp.stack(outs, axis=0)
```
